```python
import math
import jax, jax.numpy as jnp
from jax import lax
import numpy as np

D_MODEL = 1024
BATCH = 16
SEQ = 4096
DEPTH = 4

GRID_W = 64
CTX_LEN = 256
EPS = 1e-6
CONV_CH = 512
CONV_WIDTH = 31
N_HEADS = 8
HEAD_DIM = 64
V_DIM = 2 * HEAD_DIM
QK_W = N_HEADS * 2 * HEAD_DIM
ATTN_W = N_HEADS * V_DIM
Q_BLOCK = 128
ROPE_BASE = 10000.0
POOL_CH = 512
POOL_WINDOWS = (2, 4, 8, 16)
POOL_GROUP = POOL_CH // len(POOL_WINDOWS)
N_BRANCH = 3
D_FF = -(-(8 * D_MODEL) // (3 * 256)) * 256
N_MOD = 6
COL_A = 0
COL_Q = COL_A + 2 * CONV_CH
COL_K = COL_Q + QK_W
COL_V = COL_K + QK_W
COL_P = COL_V + ATTN_W
COL_G = COL_P + POOL_CH
IN_W = COL_G + N_BRANCH * D_MODEL

kernel_name = "hybrid_conv_diffattn_pool_dit_block"


def rms_norm(x, g):
    x32 = x.astype(jnp.float32)
    y = x32 * lax.rsqrt(jnp.mean(x32 * x32, axis=-1, keepdims=True) + EPS)
    return (y * g.astype(jnp.float32)).astype(x.dtype)


def layer_norm(x, g, b):
    x32 = x.astype(jnp.float32)
    mu = jnp.mean(x32, axis=-1, keepdims=True)
    var = jnp.mean(jnp.square(x32 - mu), axis=-1, keepdims=True)
    y = (x32 - mu) * lax.rsqrt(var + EPS) * g.astype(jnp.float32) + b.astype(jnp.float32)
    return y.astype(x.dtype)


def axial_rope_tables(n_tokens):
    rows = n_tokens // GRID_W
    row = jnp.repeat(jnp.arange(rows), GRID_W).astype(jnp.float32)
    col = jnp.tile(jnp.arange(GRID_W), rows).astype(jnp.float32)
    half = HEAD_DIM // 2
    inv = ROPE_BASE ** (-jnp.arange(0, half, 2, dtype=jnp.float32) / half)
    ang_r = row[:, None] * inv[None, :]
    ang_c = col[:, None] * inv[None, :]
    return (jnp.cos(ang_r), jnp.sin(ang_r), jnp.cos(ang_c), jnp.sin(ang_c))


def _rotate(x, cos, sin):
    n = x.shape[-1] // 2
    x1, x2 = x[..., :n], x[..., n:]
    cos = cos.astype(x.dtype)
    sin = sin.astype(x.dtype)
    return jnp.concatenate([x1 * cos - x2 * sin, x2 * cos + x1 * sin], axis=-1)


def apply_axial_rope(x, rope):
    cos_r, sin_r, cos_c, sin_c = rope
    half = HEAD_DIM // 2
    return jnp.concatenate([_rotate(x[..., :half], cos_r, sin_r),
                            _rotate(x[..., half:], cos_c, sin_c)], axis=-1)


def depthwise_conv(u, w, b):
    k, ch = w.shape
    out = lax.conv_general_dilated(u, w[:, None, :].astype(u.dtype), window_strides=(1,),
                                   padding=[(k // 2, k // 2)],
                                   dimension_numbers=('NWC', 'WIO', 'NWC'),
                                   feature_group_count=ch)
    return out + b.astype(u.dtype)


def conformer_conv(h, w_glu, conv_w, conv_b, ln_g, ln_b, w_proj):
    u = h @ w_glu
    u = u[..., :CONV_CH] * jax.nn.sigmoid(u[..., CONV_CH:])
    u = depthwise_conv(u, conv_w, conv_b)
    u = layer_norm(u, ln_g, ln_b)
    return jax.nn.silu(u) @ w_proj


def multiscale_pool(h, w_p, w_group, scale, w_proj):
    p = h @ w_p
    b_, l_ = p.shape[0], p.shape[1]
    cs = jnp.cumsum(p.astype(jnp.float32), axis=1)
    cs = jnp.pad(cs, ((0, 0), (1, 0), (0, 0)))
    t = jnp.arange(l_)
    outs = []
    for gi, w in enumerate(POOL_WINDOWS):
        lo = jnp.clip(t - w // 2, 0, l_)
        hi = jnp.clip(t + w // 2, 0, l_)
        csg = cs[..., gi * POOL_GROUP:(gi + 1) * POOL_GROUP]
        s = jnp.take(csg, hi, axis=1) - jnp.take(csg, lo, axis=1)
        outs.append(s / (hi - lo).astype(jnp.float32)[None, :, None])
    pooled = jnp.concatenate(outs, axis=-1).astype(p.dtype) - p
    pooled = pooled.reshape(b_, l_, len(POOL_WINDOWS), POOL_GROUP)
    mixed = jnp.einsum('blgc,gcd->blgd', pooled, w_group).reshape(b_, l_, POOL_CH)
    return (mixed * scale) @ w_proj


def diff_qkv(h, w_q, w_k, w_v):
    b_, l_ = h.shape[0], h.shape[1]
    q = (h @ w_q).reshape(b_, l_, N_HEADS, 2, HEAD_DIM).transpose(0, 2, 3, 1, 4)
    k = (h @ w_k).reshape(b_, l_, N_HEADS, 2, HEAD_DIM).transpose(0, 2, 3, 1, 4)
    v = (h @ w_v).reshape(b_, l_, N_HEADS, V_DIM).transpose(0, 2, 1, 3)
    return q, k, v


def diff_attend(q, k, v, lam):
    s = jnp.einsum('bhiqd,bhikd->bhiqk', q, k).astype(jnp.float32) * (HEAD_DIM ** -0.5)
    p = jax.nn.softmax(s, axis=-1)
    a = (p[:, :, 0] - lam * p[:, :, 1]).astype(v.dtype)
    return jnp.einsum('bhqk,bhkd->bhqd', a, v)


def diff_finish(o, subln_g, lam_init, w_proj):
    o = rms_norm(o, subln_g) * (1.0 - lam_init)
    b_, h_, l_, _ = o.shape
    return o.transpose(0, 2, 1, 3).reshape(b_, l_, ATTN_W) @ w_proj


def gated_merge(h, y_a, y_b, y_c, w_g, b_gate, w_out):
    g = jax.nn.sigmoid(h @ w_g + b_gate)
    m = (g[..., :D_MODEL] * y_a + g[..., D_MODEL:2 * D_MODEL] * y_b
         + g[..., 2 * D_MODEL:] * y_c)
    return m @ w_out


def token_mixer(h, hc, rope, need_ctx, lam_init, w_in, b_gate, conv_w, conv_b, ln_g, ln_b,
                w_conv_out, lam_q1, lam_k1, lam_q2, lam_k2, subln_g, w_attn_out,
                w_pool_group, pool_scale, w_pool_out, w_out):
    w_glu = w_in[:, COL_A:COL_Q]
    w_q = w_in[:, COL_Q:COL_K]
    w_k = w_in[:, COL_K:COL_V]
    w_v = w_in[:, COL_V:COL_P]
    w_p = w_in[:, COL_P:COL_G]
    w_g = w_in[:, COL_G:IN_W]
    lam = (jnp.exp(jnp.sum(lam_q1.astype(jnp.float32) * lam_k1.astype(jnp.float32)))
           - jnp.exp(jnp.sum(lam_q2.astype(jnp.float32) * lam_k2.astype(jnp.float32)))
           + lam_init)

    qc, kc, vc = diff_qkv(hc, w_q, w_k, w_v)
    q, k, v = diff_qkv(h, w_q, w_k, w_v)
    q = apply_axial_rope(q, rope)
    k = apply_axial_rope(k, rope)
    k_all = jnp.concatenate([kc, k], axis=3)
    v_all = jnp.concatenate([vc, v], axis=2)
    b_, h_, _, l_, _ = q.shape
    nb = l_ // Q_BLOCK
    qb = jnp.moveaxis(q.reshape(b_, h_, 2, nb, Q_BLOCK, HEAD_DIM), 3, 0)
    ob = lax.map(lambda qq: diff_attend(qq, k_all, v_all, lam), qb)
    o = jnp.moveaxis(ob, 0, 2).reshape(b_, h_, l_, V_DIM)

    y_a = conformer_conv(h, w_glu, conv_w, conv_b, ln_g, ln_b, w_conv_out)
    y_b = diff_finish(o, subln_g, lam_init, w_attn_out)
    y_c = multiscale_pool(h, w_p, w_pool_group, pool_scale, w_pool_out)
    y = gated_merge(h, y_a, y_b, y_c, w_g, b_gate, w_out)
    if not need_ctx:
        return y, None
    oc = diff_attend(qc, kc, vc, lam)
    yc_a = conformer_conv(hc, w_glu, conv_w, conv_b, ln_g, ln_b, w_conv_out)
    yc_b = diff_finish(oc, subln_g, lam_init, w_attn_out)
    yc_c = multiscale_pool(hc, w_p, w_pool_group, pool_scale, w_pool_out)
    yc = gated_merge(hc, yc_a, yc_b, yc_c, w_g, b_gate, w_out)
    return y, yc


def swiglu(h, w_ffn_in, w_ffn_out):
    u = h @ w_ffn_in
    return (jax.nn.silu(u[..., :D_FF]) * u[..., D_FF:]) @ w_ffn_out


def setup_inputs(seed: int = 0) -> dict:
    key = jax.random.key(seed)
    ks = jax.random.split(key, 32)
    f32 = jnp.float32
    nrm = lambda k, shape, s: jax.random.normal(k, shape, f32) * s
    gain = lambda k, shape: 1.0 + 0.02 * jax.random.normal(k, shape, f32)
    L = DEPTH
    return {
        "x": nrm(ks[0], (BATCH, SEQ, D_MODEL), 1.0),
        "c": nrm(ks[1], (BATCH, D_MODEL), 1.0),
        "ctx": nrm(ks[2], (BATCH, CTX_LEN, D_MODEL), 1.0),
        "c_ctx": nrm(ks[3], (D_MODEL,), 1.0),
        "w_mod": nrm(ks[4], (L, D_MODEL, N_MOD * D_MODEL), 0.5 * D_MODEL ** -0.5),
        "b_mod": nrm(ks[5], (L, N_MOD * D_MODEL), 0.02),
        "g_pre_mix": gain(ks[6], (L, D_MODEL)),
        "g_post_mix": gain(ks[7], (L, D_MODEL)),
        "w_in": nrm(ks[8], (L, D_MODEL, IN_W), D_MODEL ** -0.5),
        "b_gate": nrm(ks[9], (L, N_BRANCH * D_MODEL), 0.02),
        "conv_w": nrm(ks[10], (L, CONV_WIDTH, CONV_CH), CONV_WIDTH ** -0.5),
        "conv_b": nrm(ks[11], (L, CONV_CH), 0.02),
        "conv_ln_g": gain(ks[12], (L, CONV_CH)),
        "conv_ln_b": nrm(ks[13], (L, CONV_CH), 0.02),
        "w_conv_out": nrm(ks[14], (L, CONV_CH, D_MODEL), CONV_CH ** -0.5),
        "lam_q1": nrm(ks[15], (L, HEAD_DIM), 0.1),
        "lam_k1": nrm(ks[16], (L, HEAD_DIM), 0.1),
        "lam_q2": nrm(ks[17], (L, HEAD_DIM), 0.1),
        "lam_k2": nrm(ks[18], (L, HEAD_DIM), 0.1),
        "subln_g": gain(ks[19], (L, V_DIM)),
        "w_attn_out": nrm(ks[20], (L, ATTN_W, D_MODEL), ATTN_W ** -0.5),
        "w_pool_group": nrm(ks[21], (L, len(POOL_WINDOWS), POOL_GROUP, POOL_GROUP), POOL_GROUP ** -0.5),
        "pool_scale": 1.0 + 0.1 * jax.random.normal(ks[22], (L, POOL_CH), f32),
        "w_pool_out": nrm(ks[23], (L, POOL_CH, D_MODEL), POOL_CH ** -0.5),
        "w_out": nrm(ks[24], (L, D_MODEL, D_MODEL), D_MODEL ** -0.5),
        "g_pre_ffn": gain(ks[25], (L, D_MODEL)),
        "g_post_ffn": gain(ks[26], (L, D_MODEL)),
        "w_ffn_in": nrm(ks[27], (L, D_MODEL, 2 * D_FF), D_MODEL ** -0.5),
        "w_ffn_out": nrm(ks[28], (L, D_FF, D_MODEL), D_FF ** -0.5),
    }


def reference(x, c, ctx, c_ctx, w_mod, b_mod, g_pre_mix, g_post_mix, w_in, b_gate, conv_w,
              conv_b, conv_ln_g, conv_ln_b, w_conv_out, lam_q1, lam_k1, lam_q2, lam_k2,
              subln_g, w_attn_out, w_pool_group, pool_scale, w_pool_out, w_out, g_pre_ffn,
              g_post_ffn, w_ffn_in, w_ffn_out):
    rope = axial_rope_tables(x.shape[1])
    xl, xc = x, ctx
    for l in range(DEPTH):
        need_ctx = l < DEPTH - 1
        lam_init = 0.8 - 0.6 * math.exp(-0.3 * l)
        mod = jax.nn.silu(c) @ w_mod[l] + b_mod[l]
        modc = jax.nn.silu(c_ctx) @ w_mod[l] + b_mod[l]
        sh1, sc1, gt1, sh2, sc2, gt2 = jnp.split(mod[:, None, :], N_MOD, axis=-1)
        csh1, csc1, cgt1, csh2, csc2, cgt2 = jnp.split(modc, N_MOD, axis=-1)

        h = rms_norm(xl, g_pre_mix[l]) * (1.0 + sc1) + sh1
        hc = rms_norm(xc, g_pre_mix[l]) * (1.0 + csc1) + csh1
        y, yc = token_mixer(h, hc, rope, need_ctx, lam_init, w_in[l], b_gate[l], conv_w[l],
                            conv_b[l], conv_ln_g[l], conv_ln_b[l], w_conv_out[l], lam_q1[l],
                            lam_k1[l], lam_q2[l], lam_k2[l], subln_g[l], w_attn_out[l],
                            w_pool_group[l], pool_scale[l], w_pool_out[l], w_out[l])
        xl = xl + gt1 * rms_norm(y, g_post_mix[l])

        h = rms_norm(xl, g_pre_ffn[l]) * (1.0 + sc2) + sh2
        xl = xl + gt2 * rms_norm(swiglu(h, w_ffn_in[l], w_ffn_out[l]), g_post_ffn[l])

        if need_ctx:
            xc = xc + cgt1 * rms_norm(yc, g_post_mix[l])
            hc = rms_norm(xc, g_pre_ffn[l]) * (1.0 + csc2) + csh2
            xc = xc + cgt2 * rms_norm(swiglu(hc, w_ffn_in[l], w_ffn_out[l]), g_post_ffn[l])
    return xl
```

```python
import functools
import math

import jax
import jax.numpy as jnp
from jax import lax
from jax.experimental import pallas as pl
from jax.experimental.pallas import tpu as pltpu

F32 = jnp.float32
BF16 = jnp.bfloat16

D_MODEL = 1024
DEPTH = 4
GRID_W = 64
CTX_LEN = 256
EPS = 1e-6
CONV_CH = 512
CONV_WIDTH = 31
N_HEADS = 8
HEAD_DIM = 64
V_DIM = 2 * HEAD_DIM
QK_W = N_HEADS * 2 * HEAD_DIM
ATTN_W = N_HEADS * V_DIM
ROPE_BASE = 10000.0
POOL_CH = 512
POOL_WINDOWS = (2, 4, 8, 16)
POOL_GROUP = POOL_CH // len(POOL_WINDOWS)
N_BRANCH = 3
D_FF = -(-(8 * D_MODEL) // (3 * 256)) * 256
N_MOD = 6
COL_A = 0
COL_Q = COL_A + 2 * CONV_CH
COL_K = COL_Q + QK_W
COL_V = COL_K + QK_W
COL_P = COL_V + ATTN_W
COL_G = COL_P + POOL_CH
IN_W = COL_G + N_BRANCH * D_MODEL

LANES = 128
HALO = 16
MOD_ROWS = 24
VMEM_LIMIT = 56 * 1024 * 1024

TM_LAT = 512
TQ = 256
TK = 512
TS = 256
ROW_CHUNK = 32
FFN_CHUNKS = (1024, 1024, 768)
NEG_BIG = -1e30


def _dot(a, b):
    return jnp.dot(a, b, preferred_element_type=F32)


def _dot_nt(a, b):
    return lax.dot_general(a, b, (((1,), (1,)), ((), ())), preferred_element_type=F32)


def _const_spec(shape):
    nd = len(shape)
    return pl.BlockSpec(shape, lambda *_: (0,) * nd, pipeline_mode=pl.Buffered(1))


def _params(n_axes):
    return pltpu.CompilerParams(dimension_semantics=("arbitrary",) * n_axes,
                                vmem_limit_bytes=VMEM_LIMIT)


def _mod_spec(layer, row_fn, chunk):
    return pl.BlockSpec((None, None, 1, D_MODEL),
                        lambda b, *_: (layer, row_fn(b), 0, chunk))


def _mod_kernel(c_ref, w_ref, b_ref, o_ref):
    c = c_ref[...]
    a = (c * jax.nn.sigmoid(c)).astype(BF16)
    o_ref[...] = _dot(a, w_ref[...].astype(BF16)) + b_ref[...]


def _modulation(c_all, w_mod, b_mod):
    depth = w_mod.shape[0]
    tn = 1536
    return pl.pallas_call(
        _mod_kernel,
        grid=(depth, (N_MOD * D_MODEL) // tn),
        in_specs=[pl.BlockSpec((MOD_ROWS, D_MODEL), lambda l, j: (0, 0)),
                  pl.BlockSpec((None, D_MODEL, tn), lambda l, j: (l, 0, j)),
                  pl.BlockSpec((None, 1, tn), lambda l, j: (l, 0, j))],
        out_specs=pl.BlockSpec((None, MOD_ROWS, tn), lambda l, j: (l, 0, j)),
        out_shape=jax.ShapeDtypeStruct((depth, MOD_ROWS, N_MOD * D_MODEL), F32),
        compiler_params=_params(2),
        name="modulation",
    )(c_all, w_mod, b_mod.reshape(depth, 1, N_MOD * D_MODEL))


def _inproj_kernel(x_ref, sc_ref, sh_ref, g_ref, w_ref, wvt_ref, bg_ref, cos_ref, sa_ref, sb_ref,
                   u_ref, q_ref, k_ref, vt_ref, p_ref, gate_ref, *, tk):
    x = x_ref[...]
    ms = jnp.mean(x * x, axis=-1, keepdims=True)
    h = (x * lax.rsqrt(ms + EPS)) * g_ref[...]
    h = h * (1.0 + sc_ref[...]) + sh_ref[...]
    hb = h.astype(BF16)
    tm = x.shape[0]

    a = _dot(hb, w_ref[:, 0:2 * CONV_CH])
    u_ref[...] = a[:, :CONV_CH] * jax.nn.sigmoid(a[:, CONV_CH:])

    cos = cos_ref[...]
    sa = sa_ref[...]
    sb = sb_ref[...]
    for col, out_ref, scale in ((2 * CONV_CH, q_ref, HEAD_DIM ** -0.5),
                                (2 * CONV_CH + QK_W, k_ref, None)):
        z = _dot(hb, w_ref[:, col:col + QK_W])
        for hh in range(N_HEADS):
            zh = z[:, hh * LANES:(hh + 1) * LANES]
            r = zh * cos + pltpu.roll(zh, LANES - 16, 1) * sa + pltpu.roll(zh, 16, 1) * sb
            if scale is not None:
                r = r * scale
            out_ref[hh] = r.astype(BF16)

    vt = _dot_nt(wvt_ref[...], hb)
    for hh in range(N_HEADS):
        for c in range(tm // tk):
            vt_ref[hh, c] = vt[hh * V_DIM:(hh + 1) * V_DIM, c * tk:(c + 1) * tk].astype(BF16)

    col_p = 2 * CONV_CH + 2 * QK_W
    p_ref[...] = _dot(hb, w_ref[:, col_p:col_p + POOL_CH])
    col_g = col_p + POOL_CH
    for j in range(N_BRANCH):
        lo = col_g + j * D_MODEL
        gl = _dot(hb, w_ref[:, lo:lo + D_MODEL]) + bg_ref[:, j * D_MODEL:(j + 1) * D_MODEL]
        gate_ref[:, j * D_MODEL:(j + 1) * D_MODEL] = jax.nn.sigmoid(gl).astype(BF16)


def _inproj(x, mod, layer, row_fn, g_pre, w_rest, wvt, b_gate, rope, tm, tk):
    bsz, seq, _ = x.shape
    cos, sa, sb = rope
    n_w = w_rest.shape[1]
    tok = lambda width: pl.BlockSpec((None, tm, width), lambda b, i: (b, i, 0))
    head = pl.BlockSpec((None, N_HEADS, tm, LANES), lambda b, i: (b, 0, i, 0))
    tab = pl.BlockSpec((tm, LANES), lambda b, i: (i, 0))
    return pl.pallas_call(
        functools.partial(_inproj_kernel, tk=tk),
        grid=(bsz, seq // tm),
        in_specs=[tok(D_MODEL), _mod_spec(layer, row_fn, 1), _mod_spec(layer, row_fn, 0),
                  _const_spec((1, D_MODEL)), _const_spec((D_MODEL, n_w)),
                  _const_spec((ATTN_W, D_MODEL)), _const_spec((1, N_BRANCH * D_MODEL)),
                  tab, tab, tab],
        out_specs=[tok(CONV_CH), head, head,
                   pl.BlockSpec((None, N_HEADS, tm // tk, V_DIM, tk), lambda b, i: (b, 0, i, 0, 0)),
                   tok(POOL_CH), tok(N_BRANCH * D_MODEL)],
        out_shape=[jax.ShapeDtypeStruct((bsz, seq, CONV_CH), F32),
                   jax.ShapeDtypeStruct((bsz, N_HEADS, seq, LANES), BF16),
                   jax.ShapeDtypeStruct((bsz, N_HEADS, seq, LANES), BF16),
                   jax.ShapeDtypeStruct((bsz, N_HEADS, seq // tk, V_DIM, tk), BF16),
                   jax.ShapeDtypeStruct((bsz, seq, POOL_CH), F32),
                   jax.ShapeDtypeStruct((bsz, seq, N_BRANCH * D_MODEL), BF16)],
        compiler_params=_params(2),
        name="inproj",
    )(x, mod, mod, g_pre, w_rest, wvt, b_gate, cos, sa, sb)


def _attn_kernel(*refs, n_lat, lam_init):
    if n_lat:
        lamp_ref, gs_ref, q_ref, kc_ref, vtc_ref, k_ref, vt_ref, o_ref = refs
    else:
        lamp_ref, gs_ref, q_ref, kc_ref, vtc_ref, o_ref = refs
    tq = q_ref.shape[0]
    q = q_ref[...].astype(F32)
    lane = lax.broadcasted_iota(jnp.int32, q.shape, 1)
    first = (lane < HEAD_DIM).astype(F32)
    qs = jnp.concatenate([q * first, q * (1.0 - first)], axis=0).astype(BF16)

    def chunk(kch, vtch, carry):
        m, l, acc = carry
        s = _dot_nt(kch, qs)
        m_new = jnp.maximum(m, jnp.max(s, axis=0, keepdims=True))
        alpha = jnp.exp(m - m_new)
        p = jnp.exp(s - m_new)
        l_new = alpha * l + jnp.sum(p, axis=0, keepdims=True)
        acc_new = alpha * acc + _dot(vtch, p.astype(BF16))
        return m_new, l_new, acc_new

    carry = (jnp.full((1, 2 * tq), NEG_BIG, F32), jnp.zeros((1, 2 * tq), F32),
             jnp.zeros((V_DIM, 2 * tq), F32))
    carry = chunk(kc_ref[...], vtc_ref[...], carry)
    if n_lat:
        tk = k_ref.shape[0] // n_lat

        def body(j, c):
            start = pl.multiple_of(j * tk, tk)
            return chunk(k_ref[pl.ds(start, tk), :], vt_ref[j], c)

        carry = lax.fori_loop(0, n_lat, body, carry)
    _, l, acc = carry

    lp = lamp_ref[...]
    lam = (jnp.exp(jnp.sum(lp[0:1] * lp[1:2], axis=1, keepdims=True))
           - jnp.exp(jnp.sum(lp[2:3] * lp[3:4], axis=1, keepdims=True)) + lam_init)
    on = acc * (1.0 / l)
    o = on[:, :tq] - lam * on[:, tq:]
    ms = jnp.mean(o * o, axis=0, keepdims=True)
    o = (o * lax.rsqrt(ms + EPS)) * gs_ref[...]
    o = o * (1.0 - lam_init)
    o_ref[...] = o.T.astype(BF16)


def _attention(q, kc, vtc, k, vt, lamp, gs, lam_init):
    bsz, _, sq, _ = q.shape
    n_lat = 0 if k is None else vt.shape[2]
    qspec = pl.BlockSpec((None, None, TQ, LANES), lambda b, h, i: (b, h, i, 0))
    in_specs = [_const_spec((4, HEAD_DIM)), _const_spec((V_DIM, 1)), qspec,
                pl.BlockSpec((None, None, CTX_LEN, LANES), lambda b, h, i: (b, h, 0, 0)),
                pl.BlockSpec((None, None, None, V_DIM, CTX_LEN), lambda b, h, i: (b, h, 0, 0, 0))]
    args = [lamp, gs, q, kc, vtc]
    if n_lat:
        in_specs += [pl.BlockSpec((None, None, k.shape[2], LANES), lambda b, h, i: (b, h, 0, 0)),
                     pl.BlockSpec((None, None, n_lat, V_DIM, vt.shape[4]),
                                  lambda b, h, i: (b, h, 0, 0, 0))]
        args += [k, vt]
    return pl.pallas_call(
        functools.partial(_attn_kernel, n_lat=n_lat, lam_init=lam_init),
        grid=(bsz, N_HEADS, sq // TQ),
        in_specs=in_specs,
        out_specs=qspec,
        out_shape=jax.ShapeDtypeStruct(q.shape, BF16),
        compiler_params=_params(3),
        name="attn_lat" if n_lat else "attn_ctx",
    )(*args)


def _convpool_kernel(u_ref, up_ref, un_ref, p_ref, pp_ref, pn_ref, cw_ref, cb_ref, lng_ref,
                     lnb_ref, wg_ref, ps_ref, a_ref, c_ref, uext, pext, pooled, *, seq_len):
    i = pl.program_id(1)
    last = pl.num_programs(1) - 1
    keep_prev = (i > 0).astype(F32)
    keep_next = (i < last).astype(F32)
    for ext, cur, prev, nxt in ((uext, u_ref, up_ref, un_ref), (pext, p_ref, pp_ref, pn_ref)):
        ext[0:HALO, :] = prev[...] * keep_prev
        ext[HALO:HALO + TS, :] = cur[...]
        ext[HALO + TS:HALO + TS + HALO, :] = nxt[...] * keep_next

    half = CONV_WIDTH // 2
    for r0 in range(0, TS, ROW_CHUNK):
        acc = jnp.zeros((ROW_CHUNK, CONV_CH), F32) + cb_ref[...]
        for kk in range(CONV_WIDTH):
            start = HALO + r0 + kk - half
            acc = acc + cw_ref[kk:kk + 1, :] * uext[start:start + ROW_CHUNK, :]
        mu = jnp.mean(acc, axis=-1, keepdims=True)
        d = acc - mu
        var = jnp.mean(d * d, axis=-1, keepdims=True)
        y = d * lax.rsqrt(var + EPS) * lng_ref[...] + lnb_ref[...]
        a_ref[r0:r0 + ROW_CHUNK, :] = (y * jax.nn.sigmoid(y)).astype(BF16)

        t = (i * TS + r0 + lax.broadcasted_iota(jnp.int32, (ROW_CHUNK, 1), 0))
        for gi, w in enumerate(POOL_WINDOWS):
            lanes = slice(gi * POOL_GROUP, (gi + 1) * POOL_GROUP)
            s = jnp.zeros((ROW_CHUNK, POOL_GROUP), F32)
            for j in range(w):
                start = HALO + r0 - w // 2 + j
                s = s + pext[start:start + ROW_CHUNK, lanes]
            cnt = (jnp.minimum(t + w // 2, seq_len) - jnp.maximum(t - w // 2, 0)).astype(F32)
            centre = pext[HALO + r0:HALO + r0 + ROW_CHUNK, lanes]
            pooled[r0:r0 + ROW_CHUNK, lanes] = s / cnt - centre

    mixed = _dot(pooled[...].astype(BF16), wg_ref[...])
    c_ref[...] = (mixed * ps_ref[...]).astype(BF16)


def _convpool(u, p, conv_w, conv_b, ln_g, ln_b, wg_bd, pool_scale):
    bsz, seq, _ = u.shape
    n_halo = seq // HALO
    per = TS // HALO
    cur = lambda: pl.BlockSpec((None, TS, CONV_CH), lambda b, i: (b, i, 0))
    prev = lambda: pl.BlockSpec((None, HALO, CONV_CH),
                                lambda b, i: (b, jnp.maximum(i * per - 1, 0), 0))
    nxt = lambda: pl.BlockSpec((None, HALO, CONV_CH),
                               lambda b, i: (b, jnp.minimum((i + 1) * per, n_halo - 1), 0))
    row = lambda: _const_spec((1, CONV_CH))
    return pl.pallas_call(
        functools.partial(_convpool_kernel, seq_len=seq),
        grid=(bsz, seq // TS),
        in_specs=[cur(), prev(), nxt(), cur(), prev(), nxt(),
                  _const_spec((32, CONV_CH)), row(), row(), row(),
                  _const_spec((POOL_CH, POOL_CH)), row()],
        out_specs=[cur(), cur()],
        out_shape=[jax.ShapeDtypeStruct((bsz, seq, CONV_CH), BF16),
                   jax.ShapeDtypeStruct((bsz, seq, POOL_CH), BF16)],
        scratch_shapes=[pltpu.VMEM((TS + 2 * HALO, CONV_CH), F32),
                        pltpu.VMEM((TS + 2 * HALO, POOL_CH), F32),
                        pltpu.VMEM((TS, POOL_CH), F32)],
        compiler_params=_params(2),
        name="convpool",
    )(u, u, u, p, p, p, conv_w, conv_b, ln_g, ln_b, wg_bd, pool_scale)


def _merge_kernel(x_ref, gt_ref, gpost_ref, a_ref, o_ref, c_ref, gate_ref, wca_ref, wat_ref,
                  wpo_ref, wout_ref, xo_ref):
    ya = _dot(a_ref[...], wca_ref[...])
    o = jnp.concatenate([o_ref[hh] for hh in range(N_HEADS)], axis=1)
    yb = _dot(o, wat_ref[...])
    yc = _dot(c_ref[...], wpo_ref[...])
    m = (gate_ref[:, 0:D_MODEL] * ya + gate_ref[:, D_MODEL:2 * D_MODEL] * yb
         + gate_ref[:, 2 * D_MODEL:3 * D_MODEL] * yc)
    y = _dot(m.astype(BF16), wout_ref[...])
    ms = jnp.mean(y * y, axis=-1, keepdims=True)
    yn = (y * lax.rsqrt(ms + EPS)) * gpost_ref[...]
    xo_ref[...] = x_ref[...] + gt_ref[...] * yn


def _merge(x, mod, layer, row_fn, g_post, a, o, c, gate, wca, wat, wpo, wout, tm):
    bsz, seq, _ = x.shape
    tok = lambda width: pl.BlockSpec((None, tm, width), lambda b, i: (b, i, 0))
    return pl.pallas_call(
        _merge_kernel,
        grid=(bsz, seq // tm),
        in_specs=[tok(D_MODEL), _mod_spec(layer, row_fn, 2), _const_spec((1, D_MODEL)),
                  tok(CONV_CH),
                  pl.BlockSpec((None, N_HEADS, tm, LANES), lambda b, i: (b, 0, i, 0)),
                  tok(POOL_CH), tok(N_BRANCH * D_MODEL),
                  _const_spec((CONV_CH, D_MODEL)), _const_spec((ATTN_W, D_MODEL)),
                  _const_spec((POOL_CH, D_MODEL)), _const_spec((D_MODEL, D_MODEL))],
        out_specs=tok(D_MODEL),
        out_shape=jax.ShapeDtypeStruct(x.shape, F32),
        compiler_params=_params(2),
        name="merge",
    )(x, mod, g_post, a, o, c, gate, wca, wat, wpo, wout)


def _ffn_kernel(x_ref, sc_ref, sh_ref, gt_ref, gpre_ref, gpost_ref, w1_ref, w2_ref, xo_ref):
    x = x_ref[...]
    ms = jnp.mean(x * x, axis=-1, keepdims=True)
    h = (x * lax.rsqrt(ms + EPS)) * gpre_ref[...]
    hb = (h * (1.0 + sc_ref[...]) + sh_ref[...]).astype(BF16)
    y = jnp.zeros(x.shape, F32)
    lo = 0
    for width in FFN_CHUNKS:
        u1 = _dot(hb, w1_ref[:, lo:lo + width])
        u2 = _dot(hb, w1_ref[:, D_FF + lo:D_FF + lo + width])
        act = ((u1 * jax.nn.sigmoid(u1)) * u2).astype(BF16)
        y = y + _dot(act, w2_ref[lo:lo + width, :])
        lo += width
    ms2 = jnp.mean(y * y, axis=-1, keepdims=True)
    yn = (y * lax.rsqrt(ms2 + EPS)) * gpost_ref[...]
    xo_ref[...] = x + gt_ref[...] * yn


def _ffn(x, mod, layer, row_fn, g_pre, g_post, w1, w2, tm):
    bsz, seq, _ = x.shape
    tok = pl.BlockSpec((None, tm, D_MODEL), lambda b, i: (b, i, 0))
    return pl.pallas_call(
        _ffn_kernel,
        grid=(bsz, seq // tm),
        in_specs=[tok, _mod_spec(layer, row_fn, 4), _mod_spec(layer, row_fn, 3),
                  _mod_spec(layer, row_fn, 5), _const_spec((1, D_MODEL)), _const_spec((1, D_MODEL)),
                  _const_spec((D_MODEL, 2 * D_FF)), _const_spec((D_FF, D_MODEL))],
        out_specs=tok,
        out_shape=jax.ShapeDtypeStruct(x.shape, F32),
        compiler_params=_params(2),
        name="ffn",
    )(x, mod, mod, mod, g_pre, g_post, w1, w2)


def _rope_tables(n_tokens):
    rows = n_tokens // GRID_W
    row = jnp.repeat(jnp.arange(rows), GRID_W).astype(F32)
    col = jnp.tile(jnp.arange(GRID_W), rows).astype(F32)
    half = HEAD_DIM // 2
    inv = ROPE_BASE ** (-jnp.arange(0, half, 2, dtype=F32) / half)
    lane = jnp.arange(LANES)
    d = lane % HEAD_DIM
    freq = inv[d % (half // 2)]
    pos = jnp.where((d < half)[None, :], row[:, None], col[:, None])
    ang = pos * freq[None, :]
    first = ((d % half) < half // 2)[None, :]
    sin = jnp.sin(ang)
    return jnp.cos(ang), jnp.where(first, -sin, 0.0), jnp.where(first, 0.0, sin)


def kernel(x, c, ctx, c_ctx, w_mod, b_mod, g_pre_mix, g_post_mix, w_in, b_gate, conv_w, conv_b, conv_ln_g, conv_ln_b, w_conv_out, lam_q1, lam_k1, lam_q2, lam_k2, subln_g, w_attn_out, w_pool_group, pool_scale, w_pool_out, w_out, g_pre_ffn, g_post_ffn, w_ffn_in, w_ffn_out):
    bsz, seq, _ = x.shape
    depth = w_mod.shape[0]
    assert ctx.shape[1] == CTX_LEN and seq % TM_LAT == 0 and seq % GRID_W == 0
    assert bsz + 1 <= MOD_ROWS

    c_all = jnp.zeros((MOD_ROWS, D_MODEL), F32).at[:bsz].set(c).at[bsz].set(c_ctx)
    mod = _modulation(c_all, w_mod, b_mod).reshape(depth, MOD_ROWS, 1, N_MOD * D_MODEL)
    lat_row = lambda b: b
    ctx_row = lambda b: bsz

    rope_lat = _rope_tables(seq)
    rope_ctx = (jnp.ones((CTX_LEN, LANES), F32), jnp.zeros((CTX_LEN, LANES), F32),
                jnp.zeros((CTX_LEN, LANES), F32))
    row2 = lambda v: v.reshape(1, -1)

    xl, xc = x, ctx
    for l in range(depth):
        need_ctx = l < depth - 1
        lam_init = 0.8 - 0.6 * math.exp(-0.3 * l)
        wl = w_in[l]
        w_rest = jnp.concatenate([wl[:, :COL_V], wl[:, COL_P:]], axis=1).astype(BF16)
        wvt = wl[:, COL_V:COL_P].T.astype(BF16)
        bg = row2(b_gate[l])
        lamp = jnp.stack([lam_q1[l], lam_k1[l], lam_q2[l], lam_k2[l]])
        gs = subln_g[l].reshape(V_DIM, 1)
        cw = jnp.zeros((32, CONV_CH), F32).at[:CONV_WIDTH].set(conv_w[l])
        wg_bd = jax.scipy.linalg.block_diag(*[w_pool_group[l, g] for g in range(len(POOL_WINDOWS))]).astype(BF16)
        wca, wat = w_conv_out[l].astype(BF16), w_attn_out[l].astype(BF16)
        wpo, wout = w_pool_out[l].astype(BF16), w_out[l].astype(BF16)
        w1, w2 = w_ffn_in[l].astype(BF16), w_ffn_out[l].astype(BF16)
        gpm, gqm = row2(g_pre_mix[l]), row2(g_post_mix[l])
        gpf, gqf = row2(g_pre_ffn[l]), row2(g_post_ffn[l])
        conv_args = (cw, row2(conv_b[l]), row2(conv_ln_g[l]), row2(conv_ln_b[l]), wg_bd,
                     row2(pool_scale[l]))

        uc, qc, kc, vtc, pc, gc = _inproj(xc, mod, l, ctx_row, gpm, w_rest, wvt, bg, rope_ctx,
                                          CTX_LEN, CTX_LEN)
        ul, ql, kl, vtl, pl_, gl = _inproj(xl, mod, l, lat_row, gpm, w_rest, wvt, bg, rope_lat,
                                           TM_LAT, TK)
        o_l = _attention(ql, kc, vtc, kl, vtl, lamp, gs, lam_init)
        a_l, c_l = _convpool(ul, pl_, *conv_args)
        xl = _merge(xl, mod, l, lat_row, gqm, a_l, o_l, c_l, gl, wca, wat, wpo, wout, TM_LAT)
        xl = _ffn(xl, mod, l, lat_row, gpf, gqf, w1, w2, TM_LAT)
        if need_ctx:
            o_c = _attention(qc, kc, vtc, None, None, lamp, gs, lam_init)
            a_c, c_c = _convpool(uc, pc, *conv_args)
            xc = _merge(xc, mod, l, ctx_row, gqm, a_c, o_c, c_c, gc, wca, wat, wpo, wout, CTX_LEN)
            xc = _ffn(xc, mod, l, ctx_row, gpf, gqf, w1, w2, CTX_LEN)
    return xl
```

```python
import functools
import math

import jax
import jax.numpy as jnp
from jax import lax
from jax.experimental import pallas as pl
from jax.experimental.pallas import tpu as pltpu

F32 = jnp.float32
BF16 = jnp.bfloat16

D_MODEL = 1024
DEPTH = 4
GRID_W = 64
CTX_LEN = 256
EPS = 1e-6
CONV_CH = 512
CONV_WIDTH = 31
N_HEADS = 8
HEAD_DIM = 64
V_DIM = 2 * HEAD_DIM
QK_W = N_HEADS * 2 * HEAD_DIM
ATTN_W = N_HEADS * V_DIM
ROPE_BASE = 10000.0
POOL_CH = 512
POOL_WINDOWS = (2, 4, 8, 16)
POOL_GROUP = POOL_CH // len(POOL_WINDOWS)
N_BRANCH = 3
D_FF = -(-(8 * D_MODEL) // (3 * 256)) * 256
N_MOD = 6
COL_A = 0
COL_Q = COL_A + 2 * CONV_CH
COL_K = COL_Q + QK_W
COL_V = COL_K + QK_W
COL_P = COL_V + ATTN_W
COL_G = COL_P + POOL_CH
IN_W = COL_G + N_BRANCH * D_MODEL

LANES = 128
SUBLANES = 8
HALO = 16
MOD_ROWS = 24
VMEM_LIMIT = 56 * 1024 * 1024

TM_LAT = 512
TQ = 512
TK = 512
KEY_GROUP = 1
LOOKAHEAD = 2
TS = 256
ROW_CHUNK = 32
FFN_CHUNKS = (1024, 1024, 768)
NEG_BIG = -1e30
VT_ROWS = V_DIM + 16
LOG2E = math.log2(math.e)


def _dot(a, b):
    return jnp.dot(a, b, preferred_element_type=F32)


def _dot_nt(a, b):
    return lax.dot_general(a, b, (((1,), (1,)), ((), ())), preferred_element_type=F32)


def _const_spec(shape):
    nd = len(shape)
    return pl.BlockSpec(shape, lambda *_: (0,) * nd, pipeline_mode=pl.Buffered(1))


def _params(n_axes):
    return pltpu.CompilerParams(dimension_semantics=("arbitrary",) * n_axes,
                                vmem_limit_bytes=VMEM_LIMIT)


def _mod_spec(layer, row_fn, chunk):
    return pl.BlockSpec((None, None, 1, D_MODEL),
                        lambda b, *_: (layer, row_fn(b), 0, chunk))


def _mod_kernel(c_ref, w_ref, b_ref, o_ref):
    c = c_ref[...]
    a = (c * jax.nn.sigmoid(c)).astype(BF16)
    o_ref[...] = _dot(a, w_ref[...].astype(BF16)) + b_ref[...]


def _modulation(c_all, w_mod, b_mod):
    depth = w_mod.shape[0]
    tn = 1536
    return pl.pallas_call(
        _mod_kernel,
        grid=(depth, (N_MOD * D_MODEL) // tn),
        in_specs=[pl.BlockSpec((MOD_ROWS, D_MODEL), lambda l, j: (0, 0)),
                  pl.BlockSpec((None, D_MODEL, tn), lambda l, j: (l, 0, j)),
                  pl.BlockSpec((None, 1, tn), lambda l, j: (l, 0, j))],
        out_specs=pl.BlockSpec((None, MOD_ROWS, tn), lambda l, j: (l, 0, j)),
        out_shape=jax.ShapeDtypeStruct((depth, MOD_ROWS, N_MOD * D_MODEL), F32),
        compiler_params=_params(2),
        name="modulation",
    )(c_all, w_mod, b_mod.reshape(depth, 1, N_MOD * D_MODEL))


def _inproj_kernel(x_ref, sc_ref, sh_ref, g_ref, w_ref, wvt_ref, bg_ref, cos_ref, sa_ref, sb_ref,
                   u_ref, q_ref, k_ref, vt_ref, p_ref, gate_ref, *, tk):
    x = x_ref[...]
    ms = jnp.mean(x * x, axis=-1, keepdims=True)
    h = (x * lax.rsqrt(ms + EPS)) * g_ref[...]
    h = h * (1.0 + sc_ref[...]) + sh_ref[...]
    hb = h.astype(BF16)
    tm = x.shape[0]

    a = _dot(hb, w_ref[:, 0:2 * CONV_CH])
    u_ref[...] = a[:, :CONV_CH] * jax.nn.sigmoid(a[:, CONV_CH:])

    cos = cos_ref[...]
    sa = sa_ref[...]
    sb = sb_ref[...]
    for col, out_ref, scale in ((2 * CONV_CH, q_ref, HEAD_DIM ** -0.5 * LOG2E),
                                (2 * CONV_CH + QK_W, k_ref, None)):
        z = _dot(hb, w_ref[:, col:col + QK_W])
        for hh in range(N_HEADS):
            zh = z[:, hh * LANES:(hh + 1) * LANES]
            r = zh * cos + pltpu.roll(zh, LANES - 16, 1) * sa + pltpu.roll(zh, 16, 1) * sb
            if scale is not None:
                r = r * scale
            out_ref[hh] = r.astype(BF16)

    vt = _dot_nt(wvt_ref[...], hb)
    ones_rows = (lax.broadcasted_iota(jnp.int32, (VT_ROWS - V_DIM, tk), 0) == 0).astype(BF16)
    for hh in range(N_HEADS):
        for c in range(tm // tk):
            vt_ref[hh, c, 0:V_DIM, :] = vt[hh * V_DIM:(hh + 1) * V_DIM,
                                           c * tk:(c + 1) * tk].astype(BF16)
            vt_ref[hh, c, V_DIM:VT_ROWS, :] = ones_rows

    col_p = 2 * CONV_CH + 2 * QK_W
    p_ref[...] = _dot(hb, w_ref[:, col_p:col_p + POOL_CH])
    col_g = col_p + POOL_CH
    for j in range(N_BRANCH):
        lo = col_g + j * D_MODEL
        gl = _dot(hb, w_ref[:, lo:lo + D_MODEL]) + bg_ref[:, j * D_MODEL:(j + 1) * D_MODEL]
        gate_ref[:, j * D_MODEL:(j + 1) * D_MODEL] = jax.nn.sigmoid(gl).astype(BF16)


def _inproj(x, mod, layer, row_fn, g_pre, w_rest, wvt, b_gate, rope, tm, tk):
    bsz, seq, _ = x.shape
    cos, sa, sb = rope
    n_w = w_rest.shape[1]
    tok = lambda width: pl.BlockSpec((None, tm, width), lambda b, i: (b, i, 0))
    head = pl.BlockSpec((None, N_HEADS, tm, LANES), lambda b, i: (b, 0, i, 0))
    tab = pl.BlockSpec((tm, LANES), lambda b, i: (i, 0))
    return pl.pallas_call(
        functools.partial(_inproj_kernel, tk=tk),
        grid=(bsz, seq // tm),
        in_specs=[tok(D_MODEL), _mod_spec(layer, row_fn, 1), _mod_spec(layer, row_fn, 0),
                  _const_spec((1, D_MODEL)), _const_spec((D_MODEL, n_w)),
                  _const_spec((ATTN_W, D_MODEL)), _const_spec((1, N_BRANCH * D_MODEL)),
                  tab, tab, tab],
        out_specs=[tok(CONV_CH), head, head,
                   pl.BlockSpec((None, N_HEADS, tm // tk, VT_ROWS, tk), lambda b, i: (b, 0, i, 0, 0)),
                   tok(POOL_CH), tok(N_BRANCH * D_MODEL)],
        out_shape=[jax.ShapeDtypeStruct((bsz, seq, CONV_CH), F32),
                   jax.ShapeDtypeStruct((bsz, N_HEADS, seq, LANES), BF16),
                   jax.ShapeDtypeStruct((bsz, N_HEADS, seq, LANES), BF16),
                   jax.ShapeDtypeStruct((bsz, N_HEADS, seq // tk, VT_ROWS, tk), BF16),
                   jax.ShapeDtypeStruct((bsz, seq, POOL_CH), F32),
                   jax.ShapeDtypeStruct((bsz, seq, N_BRANCH * D_MODEL), BF16)],
        compiler_params=_params(2),
        name="inproj",
    )(x, mod, mod, g_pre, w_rest, wvt, b_gate, cos, sa, sb)


def _attn_kernel(*refs, n_lat, lam_init):
    if n_lat:
        lamp_ref, gs_ref, q_ref, kc_ref, vtc_ref, k_ref, vt_ref, o_ref = refs
    else:
        lamp_ref, gs_ref, q_ref, kc_ref, vtc_ref, o_ref = refs
    tq = q_ref.shape[0]
    q = q_ref[...].astype(F32)
    lane = lax.broadcasted_iota(jnp.int32, q.shape, 1)
    first = (lane < HEAD_DIM).astype(F32)
    qs = jnp.concatenate([q * first, q * (1.0 - first)], axis=0).astype(BF16)

    def scores(keys):
        return [_dot_nt(kch, qs) for kch in keys]

    def update(ss, vts, carry):
        m, acc = carry
        m_new = m
        for s in ss:
            m_new = jnp.maximum(m_new, jnp.max(s, axis=0, keepdims=True))
        acc_new = jnp.exp2(m - m_new) * acc
        for s, vtch in zip(ss, vts):
            acc_new = acc_new + _dot(vtch, jnp.exp2(s - m_new).astype(BF16))
        return m_new, acc_new

    steps = [([kc_ref], [vtc_ref])]
    if n_lat:
        tk = k_ref.shape[0] // n_lat
        for j0 in range(0, n_lat, KEY_GROUP):
            js = range(j0, j0 + KEY_GROUP)
            steps.append(([k_ref.at[j * tk:(j + 1) * tk, :] for j in js],
                          [vt_ref.at[j] for j in js]))
    carry = (jnp.full((1, 2 * tq), NEG_BIG, F32), jnp.zeros((VT_ROWS, 2 * tq), F32))
    queue = [scores([r[...] for r in steps[i][0]]) for i in range(min(LOOKAHEAD, len(steps)))]
    for i, (_, vts) in enumerate(steps):
        if i + LOOKAHEAD < len(steps):
            queue.append(scores([r[...] for r in steps[i + LOOKAHEAD][0]]))
        carry = update(queue.pop(0), [r[...] for r in vts], carry)
    _, acc = carry

    lp = lamp_ref[...]
    lam = (jnp.exp(jnp.sum(lp[0:1] * lp[1:2], axis=1, keepdims=True))
           - jnp.exp(jnp.sum(lp[2:3] * lp[3:4], axis=1, keepdims=True)) + lam_init)
    on = acc[0:V_DIM] * (1.0 / acc[V_DIM:V_DIM + 1])
    o = on[:, :tq] - lam * on[:, tq:]
    ms = jnp.mean(o * o, axis=0, keepdims=True)
    o = (o * lax.rsqrt(ms + EPS)) * gs_ref[...]
    o = o * (1.0 - lam_init)
    o_ref[...] = o.T.astype(BF16)


def _attention(q, kc, vtc, k, vt, lamp, gs, lam_init):
    bsz, _, sq, _ = q.shape
    n_lat = 0 if k is None else vt.shape[2]
    tq = min(TQ, sq)
    qspec = pl.BlockSpec((None, None, tq, LANES), lambda b, h, i: (b, h, i, 0))
    in_specs = [_const_spec((4, HEAD_DIM)), _const_spec((V_DIM, 1)), qspec,
                pl.BlockSpec((None, None, CTX_LEN, LANES), lambda b, h, i: (b, h, 0, 0)),
                pl.BlockSpec((None, None, None, VT_ROWS, CTX_LEN), lambda b, h, i: (b, h, 0, 0, 0))]
    args = [lamp, gs, q, kc, vtc]
    if n_lat:
        in_specs += [pl.BlockSpec((None, None, k.shape[2], LANES), lambda b, h, i: (b, h, 0, 0)),
                     pl.BlockSpec((None, None, n_lat, VT_ROWS, vt.shape[4]),
                                  lambda b, h, i: (b, h, 0, 0, 0))]
        args += [k, vt]
    return pl.pallas_call(
        functools.partial(_attn_kernel, n_lat=n_lat, lam_init=lam_init),
        grid=(bsz, N_HEADS, sq // tq),
        in_specs=in_specs,
        out_specs=qspec,
        out_shape=jax.ShapeDtypeStruct(q.shape, BF16),
        compiler_params=_params(3),
        name="attn_lat" if n_lat else "attn_ctx",
    )(*args)


def _convpool_kernel(u_ref, up_ref, un_ref, p_ref, pp_ref, pn_ref, cw_ref, cb_ref, lng_ref,
                     lnb_ref, wg_ref, ps_ref, a_ref, c_ref, uext, pext, pooled, ush, *, seq_len):
    i = pl.program_id(1)
    last = pl.num_programs(1) - 1
    keep_prev = (i > 0).astype(F32)
    keep_next = (i < last).astype(F32)
    for ext, cur, prev, nxt in ((uext, u_ref, up_ref, un_ref), (pext, p_ref, pp_ref, pn_ref)):
        ext[0:HALO, :] = prev[...] * keep_prev
        ext[HALO:HALO + TS, :] = cur[...]
        ext[HALO + TS:HALO + TS + HALO, :] = nxt[...] * keep_next

    n_sh = TS + 2 * HALO - SUBLANES
    for r in range(1, SUBLANES):
        ush[r - 1] = uext[r:r + n_sh, :]

    half = CONV_WIDTH // 2
    for r0 in range(0, TS, ROW_CHUNK):
        acc = jnp.zeros((ROW_CHUNK, CONV_CH), F32) + cb_ref[...]
        for kk in range(CONV_WIDTH):
            off = HALO + kk - half
            base = r0 + (off // SUBLANES) * SUBLANES
            if off % SUBLANES:
                rows = ush[off % SUBLANES - 1, base:base + ROW_CHUNK, :]
            else:
                rows = uext[base:base + ROW_CHUNK, :]
            acc = acc + cw_ref[kk:kk + 1, :] * rows
        mu = jnp.mean(acc, axis=-1, keepdims=True)
        d = acc - mu
        var = jnp.mean(d * d, axis=-1, keepdims=True)
        y = d * lax.rsqrt(var + EPS) * lng_ref[...] + lnb_ref[...]
        a_ref[r0:r0 + ROW_CHUNK, :] = (y * jax.nn.sigmoid(y)).astype(BF16)

        t = (i * TS + r0 + lax.broadcasted_iota(jnp.int32, (ROW_CHUNK, 1), 0))
        for gi, w in enumerate(POOL_WINDOWS):
            lanes = slice(gi * POOL_GROUP, (gi + 1) * POOL_GROUP)
            s = jnp.zeros((ROW_CHUNK, POOL_GROUP), F32)
            for j in range(w):
                start = HALO + r0 - w // 2 + j
                s = s + pext[start:start + ROW_CHUNK, lanes]
            cnt = (jnp.minimum(t + w // 2, seq_len) - jnp.maximum(t - w // 2, 0)).astype(F32)
            centre = pext[HALO + r0:HALO + r0 + ROW_CHUNK, lanes]
            pooled[r0:r0 + ROW_CHUNK, lanes] = s / cnt - centre

    mixed = _dot(pooled[...].astype(BF16), wg_ref[...])
    c_ref[...] = (mixed * ps_ref[...]).astype(BF16)


def _convpool(u, p, conv_w, conv_b, ln_g, ln_b, wg_bd, pool_scale):
    bsz, seq, _ = u.shape
    n_halo = seq // HALO
    per = TS // HALO
    cur = lambda: pl.BlockSpec((None, TS, CONV_CH), lambda b, i: (b, i, 0))
    prev = lambda: pl.BlockSpec((None, HALO, CONV_CH),
                                lambda b, i: (b, jnp.maximum(i * per - 1, 0), 0))
    nxt = lambda: pl.BlockSpec((None, HALO, CONV_CH),
                               lambda b, i: (b, jnp.minimum((i + 1) * per, n_halo - 1), 0))
    row = lambda: _const_spec((1, CONV_CH))
    return pl.pallas_call(
        functools.partial(_convpool_kernel, seq_len=seq),
        grid=(bsz, seq // TS),
        in_specs=[cur(), prev(), nxt(), cur(), prev(), nxt(),
                  _const_spec((32, CONV_CH)), row(), row(), row(),
                  _const_spec((POOL_CH, POOL_CH)), row()],
        out_specs=[cur(), cur()],
        out_shape=[jax.ShapeDtypeStruct((bsz, seq, CONV_CH), BF16),
                   jax.ShapeDtypeStruct((bsz, seq, POOL_CH), BF16)],
        scratch_shapes=[pltpu.VMEM((TS + 2 * HALO, CONV_CH), F32),
                        pltpu.VMEM((TS + 2 * HALO, POOL_CH), F32),
                        pltpu.VMEM((TS, POOL_CH), F32),
                        pltpu.VMEM((SUBLANES - 1, TS + 2 * HALO - SUBLANES, CONV_CH), F32)],
        compiler_params=_params(2),
        name="convpool",
    )(u, u, u, p, p, p, conv_w, conv_b, ln_g, ln_b, wg_bd, pool_scale)


def _merge_kernel(x_ref, gt_ref, gpost_ref, a_ref, o_ref, c_ref, gate_ref, wca_ref, wat_ref,
                  wpo_ref, wout_ref, xo_ref):
    ya = _dot(a_ref[...], wca_ref[...])
    o = jnp.concatenate([o_ref[hh] for hh in range(N_HEADS)], axis=1)
    yb = _dot(o, wat_ref[...])
    yc = _dot(c_ref[...], wpo_ref[...])
    m = (gate_ref[:, 0:D_MODEL] * ya + gate_ref[:, D_MODEL:2 * D_MODEL] * yb
         + gate_ref[:, 2 * D_MODEL:3 * D_MODEL] * yc)
    y = _dot(m.astype(BF16), wout_ref[...])
    ms = jnp.mean(y * y, axis=-1, keepdims=True)
    yn = (y * lax.rsqrt(ms + EPS)) * gpost_ref[...]
    xo_ref[...] = x_ref[...] + gt_ref[...] * yn


def _merge(x, mod, layer, row_fn, g_post, a, o, c, gate, wca, wat, wpo, wout, tm):
    bsz, seq, _ = x.shape
    tok = lambda width: pl.BlockSpec((None, tm, width), lambda b, i: (b, i, 0))
    return pl.pallas_call(
        _merge_kernel,
        grid=(bsz, seq // tm),
        in_specs=[tok(D_MODEL), _mod_spec(layer, row_fn, 2), _const_spec((1, D_MODEL)),
                  tok(CONV_CH),
                  pl.BlockSpec((None, N_HEADS, tm, LANES), lambda b, i: (b, 0, i, 0)),
                  tok(POOL_CH), tok(N_BRANCH * D_MODEL),
                  _const_spec((CONV_CH, D_MODEL)), _const_spec((ATTN_W, D_MODEL)),
                  _const_spec((POOL_CH, D_MODEL)), _const_spec((D_MODEL, D_MODEL))],
        out_specs=tok(D_MODEL),
        out_shape=jax.ShapeDtypeStruct(x.shape, F32),
        compiler_params=_params(2),
        name="merge",
    )(x, mod, g_post, a, o, c, gate, wca, wat, wpo, wout)


def _ffn_kernel(x_ref, sc_ref, sh_ref, gt_ref, gpre_ref, gpost_ref, w1_ref, w2_ref, xo_ref):
    x = x_ref[...]
    ms = jnp.mean(x * x, axis=-1, keepdims=True)
    h = (x * lax.rsqrt(ms + EPS)) * gpre_ref[...]
    hb = (h * (1.0 + sc_ref[...]) + sh_ref[...]).astype(BF16)
    y = jnp.zeros(x.shape, F32)
    lo = 0
    for width in FFN_CHUNKS:
        u1 = _dot(hb, w1_ref[:, lo:lo + width])
        u2 = _dot(hb, w1_ref[:, D_FF + lo:D_FF + lo + width])
        act = ((u1 * jax.nn.sigmoid(u1)) * u2).astype(BF16)
        y = y + _dot(act, w2_ref[lo:lo + width, :])
        lo += width
    ms2 = jnp.mean(y * y, axis=-1, keepdims=True)
    yn = (y * lax.rsqrt(ms2 + EPS)) * gpost_ref[...]
    xo_ref[...] = x + gt_ref[...] * yn


def _ffn(x, mod, layer, row_fn, g_pre, g_post, w1, w2, tm):
    bsz, seq, _ = x.shape
    tok = pl.BlockSpec((None, tm, D_MODEL), lambda b, i: (b, i, 0))
    return pl.pallas_call(
        _ffn_kernel,
        grid=(bsz, seq // tm),
        in_specs=[tok, _mod_spec(layer, row_fn, 4), _mod_spec(layer, row_fn, 3),
                  _mod_spec(layer, row_fn, 5), _const_spec((1, D_MODEL)), _const_spec((1, D_MODEL)),
                  _const_spec((D_MODEL, 2 * D_FF)), _const_spec((D_FF, D_MODEL))],
        out_specs=tok,
        out_shape=jax.ShapeDtypeStruct(x.shape, F32),
        compiler_params=_params(2),
        name="ffn",
    )(x, mod, mod, mod, g_pre, g_post, w1, w2)


def _rope_tables(n_tokens):
    rows = n_tokens // GRID_W
    row = jnp.repeat(jnp.arange(rows), GRID_W).astype(F32)
    col = jnp.tile(jnp.arange(GRID_W), rows).astype(F32)
    half = HEAD_DIM // 2
    inv = ROPE_BASE ** (-jnp.arange(0, half, 2, dtype=F32) / half)
    lane = jnp.arange(LANES)
    d = lane % HEAD_DIM
    freq = inv[d % (half // 2)]
    pos = jnp.where((d < half)[None, :], row[:, None], col[:, None])
    ang = pos * freq[None, :]
    first = ((d % half) < half // 2)[None, :]
    sin = jnp.sin(ang)
    return jnp.cos(ang), jnp.where(first, -sin, 0.0), jnp.where(first, 0.0, sin)


def kernel(x, c, ctx, c_ctx, w_mod, b_mod, g_pre_mix, g_post_mix, w_in, b_gate, conv_w, conv_b, conv_ln_g, conv_ln_b, w_conv_out, lam_q1, lam_k1, lam_q2, lam_k2, subln_g, w_attn_out, w_pool_group, pool_scale, w_pool_out, w_out, g_pre_ffn, g_post_ffn, w_ffn_in, w_ffn_out):
    bsz, seq, _ = x.shape
    depth = w_mod.shape[0]
    assert ctx.shape[1] == CTX_LEN and seq % TM_LAT == 0 and seq % GRID_W == 0
    assert bsz + 1 <= MOD_ROWS

    c_all = jnp.zeros((MOD_ROWS, D_MODEL), F32).at[:bsz].set(c).at[bsz].set(c_ctx)
    mod = _modulation(c_all, w_mod, b_mod).reshape(depth, MOD_ROWS, 1, N_MOD * D_MODEL)
    lat_row = lambda b: b
    ctx_row = lambda b: bsz

    rope_lat = _rope_tables(seq)
    rope_ctx = (jnp.ones((CTX_LEN, LANES), F32), jnp.zeros((CTX_LEN, LANES), F32),
                jnp.zeros((CTX_LEN, LANES), F32))
    row2 = lambda v: v.reshape(1, -1)

    xl, xc = x, ctx
    for l in range(depth):
        need_ctx = l < depth - 1
        lam_init = 0.8 - 0.6 * math.exp(-0.3 * l)
        wl = w_in[l]
        w_rest = jnp.concatenate([wl[:, :COL_V], wl[:, COL_P:]], axis=1).astype(BF16)
        wvt = wl[:, COL_V:COL_P].T.astype(BF16)
        bg = row2(b_gate[l])
        lamp = jnp.stack([lam_q1[l], lam_k1[l], lam_q2[l], lam_k2[l]])
        gs = subln_g[l].reshape(V_DIM, 1)
        cw = jnp.zeros((32, CONV_CH), F32).at[:CONV_WIDTH].set(conv_w[l])
        wg_bd = jax.scipy.linalg.block_diag(*[w_pool_group[l, g] for g in range(len(POOL_WINDOWS))]).astype(BF16)
        wca, wat = w_conv_out[l].astype(BF16), w_attn_out[l].astype(BF16)
        wpo, wout = w_pool_out[l].astype(BF16), w_out[l].astype(BF16)
        w1, w2 = w_ffn_in[l].astype(BF16), w_ffn_out[l].astype(BF16)
        gpm, gqm = row2(g_pre_mix[l]), row2(g_post_mix[l])
        gpf, gqf = row2(g_pre_ffn[l]), row2(g_post_ffn[l])
        conv_args = (cw, row2(conv_b[l]), row2(conv_ln_g[l]), row2(conv_ln_b[l]), wg_bd,
                     row2(pool_scale[l]))

        uc, qc, kc, vtc, pc, gc = _inproj(xc, mod, l, ctx_row, gpm, w_rest, wvt, bg, rope_ctx,
                                          CTX_LEN, CTX_LEN)
        ul, ql, kl, vtl, pl_, gl = _inproj(xl, mod, l, lat_row, gpm, w_rest, wvt, bg, rope_lat,
                                           TM_LAT, TK)
        o_l = _attention(ql, kc, vtc, kl, vtl, lamp, gs, lam_init)
        a_l, c_l = _convpool(ul, pl_, *conv_args)
        xl = _merge(xl, mod, l, lat_row, gqm, a_l, o_l, c_l, gl, wca, wat, wpo, wout, TM_LAT)
        xl = _ffn(xl, mod, l, lat_row, gpf, gqf, w1, w2, TM_LAT)
        if need_ctx:
            o_c = _attention(qc, kc, vtc, None, None, lamp, gs, lam_init)
            a_c, c_c = _convpool(uc, pc, *conv_args)
            xc = _merge(xc, mod, l, ctx_row, gqm, a_c, o_c, c_c, gc, wca, wat, wpo, wout, CTX_LEN)
            xc = _ffn(xc, mod, l, ctx_row, gpf, gqf, w1, w2, CTX_LEN)
    return xl
```

```python
import functools
import math

import jax
import jax.numpy as jnp
from jax import lax
from jax.experimental import pallas as pl
from jax.experimental.pallas import tpu as pltpu

F32 = jnp.float32
BF16 = jnp.bfloat16

D_MODEL = 1024
DEPTH = 4
GRID_W = 64
CTX_LEN = 256
EPS = 1e-6
CONV_CH = 512
CONV_WIDTH = 31
N_HEADS = 8
HEAD_DIM = 64
V_DIM = 2 * HEAD_DIM
QK_W = N_HEADS * 2 * HEAD_DIM
ATTN_W = N_HEADS * V_DIM
ROPE_BASE = 10000.0
POOL_CH = 512
POOL_WINDOWS = (2, 4, 8, 16)
POOL_GROUP = POOL_CH // len(POOL_WINDOWS)
N_BRANCH = 3
D_FF = -(-(8 * D_MODEL) // (3 * 256)) * 256
N_MOD = 6
COL_A = 0
COL_Q = COL_A + 2 * CONV_CH
COL_K = COL_Q + QK_W
COL_V = COL_K + QK_W
COL_P = COL_V + ATTN_W
COL_G = COL_P + POOL_CH
IN_W = COL_G + N_BRANCH * D_MODEL

LANES = 128
SUBLANES = 8
HALO = 16
MOD_ROWS = 24
VMEM_LIMIT = 56 * 1024 * 1024

TM_LAT = 512
TQ = 512
TK = 512
KEY_GROUP = 1
LOOKAHEAD = 2
TS = 256
ROW_CHUNK = 32
FFN_CHUNKS = (1024, 1024, 768)
NEG_BIG = -1e30
VT_ROWS = V_DIM + 16
LOG2E = math.log2(math.e)


def _dot(a, b):
    return jnp.dot(a, b, preferred_element_type=F32)


def _dot_nt(a, b):
    return lax.dot_general(a, b, (((1,), (1,)), ((), ())), preferred_element_type=F32)


def _const_spec(shape):
    nd = len(shape)
    return pl.BlockSpec(shape, lambda *_: (0,) * nd, pipeline_mode=pl.Buffered(1))


def _params(n_axes):
    return pltpu.CompilerParams(dimension_semantics=("arbitrary",) * n_axes,
                                vmem_limit_bytes=VMEM_LIMIT)


def _mod_spec(layer, row_fn, chunk):
    return pl.BlockSpec((None, None, 1, D_MODEL),
                        lambda b, *_: (layer, row_fn(b), 0, chunk))


def _mod_kernel(c_ref, w_ref, b_ref, o_ref):
    c = c_ref[...]
    a = (c * jax.nn.sigmoid(c)).astype(BF16)
    o_ref[...] = _dot(a, w_ref[...].astype(BF16)) + b_ref[...]


def _modulation(c_all, w_mod, b_mod):
    depth = w_mod.shape[0]
    tn = 1536
    return pl.pallas_call(
        _mod_kernel,
        grid=(depth, (N_MOD * D_MODEL) // tn),
        in_specs=[pl.BlockSpec((MOD_ROWS, D_MODEL), lambda l, j: (0, 0)),
                  pl.BlockSpec((None, D_MODEL, tn), lambda l, j: (l, 0, j)),
                  pl.BlockSpec((None, 1, tn), lambda l, j: (l, 0, j))],
        out_specs=pl.BlockSpec((None, MOD_ROWS, tn), lambda l, j: (l, 0, j)),
        out_shape=jax.ShapeDtypeStruct((depth, MOD_ROWS, N_MOD * D_MODEL), F32),
        compiler_params=_params(2),
        name="modulation",
    )(c_all, w_mod, b_mod.reshape(depth, 1, N_MOD * D_MODEL))


def _inproj_kernel(x_ref, sc_ref, sh_ref, g_ref, w_ref, wvt_ref, bg_ref, cos_ref, sa_ref, sb_ref,
                   u_ref, q_ref, k_ref, vt_ref, p_ref, gate_ref, *, tk):
    x = x_ref[...]
    ms = jnp.mean(x * x, axis=-1, keepdims=True)
    h = (x * lax.rsqrt(ms + EPS)) * g_ref[...]
    h = h * (1.0 + sc_ref[...]) + sh_ref[...]
    hb = h.astype(BF16)
    tm = x.shape[0]

    a = _dot(hb, w_ref[:, 0:2 * CONV_CH])
    u_ref[...] = a[:, :CONV_CH] * jax.nn.sigmoid(a[:, CONV_CH:])

    cos = cos_ref[...]
    sa = sa_ref[...]
    sb = sb_ref[...]
    for col, out_ref, scale in ((2 * CONV_CH, q_ref, HEAD_DIM ** -0.5 * LOG2E),
                                (2 * CONV_CH + QK_W, k_ref, None)):
        z = _dot(hb, w_ref[:, col:col + QK_W])
        for hh in range(N_HEADS):
            zh = z[:, hh * LANES:(hh + 1) * LANES]
            r = zh * cos + pltpu.roll(zh, LANES - 16, 1) * sa + pltpu.roll(zh, 16, 1) * sb
            if scale is not None:
                r = r * scale
            out_ref[hh] = r.astype(BF16)

    vt = _dot_nt(wvt_ref[...], hb)
    ones_rows = (lax.broadcasted_iota(jnp.int32, (VT_ROWS - V_DIM, tk), 0) == 0).astype(BF16)
    for hh in range(N_HEADS):
        for c in range(tm // tk):
            vt_ref[hh, c, 0:V_DIM, :] = vt[hh * V_DIM:(hh + 1) * V_DIM,
                                           c * tk:(c + 1) * tk].astype(BF16)
            vt_ref[hh, c, V_DIM:VT_ROWS, :] = ones_rows

    col_p = 2 * CONV_CH + 2 * QK_W
    p_ref[...] = _dot(hb, w_ref[:, col_p:col_p + POOL_CH])
    col_g = col_p + POOL_CH
    for j in range(N_BRANCH):
        lo = col_g + j * D_MODEL
        gl = _dot(hb, w_ref[:, lo:lo + D_MODEL]) + bg_ref[:, j * D_MODEL:(j + 1) * D_MODEL]
        gate_ref[:, j * D_MODEL:(j + 1) * D_MODEL] = jax.nn.sigmoid(gl).astype(BF16)


def _inproj(x, mod, layer, row_fn, g_pre, w_rest, wvt, b_gate, rope, tm, tk):
    bsz, seq, _ = x.shape
    cos, sa, sb = rope
    n_w = w_rest.shape[1]
    tok = lambda width: pl.BlockSpec((None, tm, width), lambda b, i: (b, i, 0))
    head = pl.BlockSpec((None, N_HEADS, tm, LANES), lambda b, i: (b, 0, i, 0))
    tab = pl.BlockSpec((tm, LANES), lambda b, i: (i, 0))
    return pl.pallas_call(
        functools.partial(_inproj_kernel, tk=tk),
        grid=(bsz, seq // tm),
        in_specs=[tok(D_MODEL), _mod_spec(layer, row_fn, 1), _mod_spec(layer, row_fn, 0),
                  _const_spec((1, D_MODEL)), _const_spec((D_MODEL, n_w)),
                  _const_spec((ATTN_W, D_MODEL)), _const_spec((1, N_BRANCH * D_MODEL)),
                  tab, tab, tab],
        out_specs=[tok(CONV_CH), head, head,
                   pl.BlockSpec((None, N_HEADS, tm // tk, VT_ROWS, tk), lambda b, i: (b, 0, i, 0, 0)),
                   tok(POOL_CH), tok(N_BRANCH * D_MODEL)],
        out_shape=[jax.ShapeDtypeStruct((bsz, seq, CONV_CH), F32),
                   jax.ShapeDtypeStruct((bsz, N_HEADS, seq, LANES), BF16),
                   jax.ShapeDtypeStruct((bsz, N_HEADS, seq, LANES), BF16),
                   jax.ShapeDtypeStruct((bsz, N_HEADS, seq // tk, VT_ROWS, tk), BF16),
                   jax.ShapeDtypeStruct((bsz, seq, POOL_CH), F32),
                   jax.ShapeDtypeStruct((bsz, seq, N_BRANCH * D_MODEL), BF16)],
        compiler_params=_params(2),
        name="inproj",
    )(x, mod, mod, g_pre, w_rest, wvt, b_gate, cos, sa, sb)


def _attn_kernel(*refs, n_lat, lam_init):
    if n_lat:
        lamp_ref, gs_ref, q_ref, kc_ref, vtc_ref, k_ref, vt_ref, o_ref = refs
    else:
        lamp_ref, gs_ref, q_ref, kc_ref, vtc_ref, o_ref = refs
    tq = q_ref.shape[0]
    q = q_ref[...].astype(F32)
    lane = lax.broadcasted_iota(jnp.int32, q.shape, 1)
    first = (lane < HEAD_DIM).astype(F32)
    qs = jnp.concatenate([q * first, q * (1.0 - first)], axis=0).astype(BF16)

    def scores(keys):
        return [_dot_nt(kch, qs) for kch in keys]

    steps = [([kc_ref], [vtc_ref])]
    if n_lat:
        tk = k_ref.shape[0] // n_lat
        for j0 in range(0, n_lat, KEY_GROUP):
            js = range(j0, j0 + KEY_GROUP)
            steps.append(([k_ref.at[j * tk:(j + 1) * tk, :] for j in js],
                          [vt_ref.at[j] for j in js]))

    def pipelined(lookahead, consume, state):
        queue = [scores([r[...] for r in steps[i][0]]) for i in range(min(lookahead, len(steps)))]
        for i, (_, vts) in enumerate(steps):
            if i + lookahead < len(steps):
                queue.append(scores([r[...] for r in steps[i + lookahead][0]]))
            state = consume(state, queue.pop(0), [r[...] for r in vts])
        return state

    def online_step(state, ss, vts):
        m, acc = state
        m_new = m
        for s in ss:
            m_new = jnp.maximum(m_new, jnp.max(s, axis=0, keepdims=True))
        acc_new = jnp.exp2(m - m_new) * acc
        for s, vtch in zip(ss, vts):
            acc_new = acc_new + _dot(vtch, jnp.exp2(s - m_new).astype(BF16))
        return m_new, acc_new

    def exact():
        init = (jnp.full((1, 2 * tq), NEG_BIG, F32), jnp.zeros((VT_ROWS, 2 * tq), F32))
        return pipelined(LOOKAHEAD, online_step, init)[1]

    def fixed_step(state, ss, vts):
        m_ref, acc = state
        if m_ref is None:
            m_ref = jnp.max(ss[0], axis=0, keepdims=True)
        for s, vtch in zip(ss, vts):
            acc = acc + _dot(vtch, jnp.exp2(s - m_ref).astype(BF16))
        return m_ref, acc

    def optimistic():
        return pipelined(LOOKAHEAD, fixed_step,
                         (None, jnp.zeros((VT_ROWS, 2 * tq), F32)))[1]

    def finish(acc):
        lp = lamp_ref[...]
        lam = (jnp.exp(jnp.sum(lp[0:1] * lp[1:2], axis=1, keepdims=True))
               - jnp.exp(jnp.sum(lp[2:3] * lp[3:4], axis=1, keepdims=True)) + lam_init)
        on = acc[0:V_DIM] * (1.0 / acc[V_DIM:V_DIM + 1])
        o = on[:, :tq] - lam * on[:, tq:]
        ms = jnp.mean(o * o, axis=0, keepdims=True)
        o = (o * lax.rsqrt(ms + EPS)) * gs_ref[...]
        o = o * (1.0 - lam_init)
        o_ref[...] = o.T.astype(BF16)

    if n_lat:
        acc = optimistic()
        finish(acc)
        overflowed = jnp.logical_not(jnp.sum(acc * 0.0) == 0.0)

        @pl.when(overflowed)
        def _():
            finish(exact())
    else:
        finish(exact())


def _attention(q, kc, vtc, k, vt, lamp, gs, lam_init):
    bsz, _, sq, _ = q.shape
    n_lat = 0 if k is None else vt.shape[2]
    tq = min(TQ, sq)
    qspec = pl.BlockSpec((None, None, tq, LANES), lambda b, h, i: (b, h, i, 0))
    in_specs = [_const_spec((4, HEAD_DIM)), _const_spec((V_DIM, 1)), qspec,
                pl.BlockSpec((None, None, CTX_LEN, LANES), lambda b, h, i: (b, h, 0, 0)),
                pl.BlockSpec((None, None, None, VT_ROWS, CTX_LEN), lambda b, h, i: (b, h, 0, 0, 0))]
    args = [lamp, gs, q, kc, vtc]
    if n_lat:
        in_specs += [pl.BlockSpec((None, None, k.shape[2], LANES), lambda b, h, i: (b, h, 0, 0)),
                     pl.BlockSpec((None, None, n_lat, VT_ROWS, vt.shape[4]),
                                  lambda b, h, i: (b, h, 0, 0, 0))]
        args += [k, vt]
    return pl.pallas_call(
        functools.partial(_attn_kernel, n_lat=n_lat, lam_init=lam_init),
        grid=(bsz, N_HEADS, sq // tq),
        in_specs=in_specs,
        out_specs=qspec,
        out_shape=jax.ShapeDtypeStruct(q.shape, BF16),
        compiler_params=_params(3),
        name="attn_lat" if n_lat else "attn_ctx",
    )(*args)


def _convpool_kernel(u_ref, up_ref, un_ref, p_ref, pp_ref, pn_ref, cw_ref, cb_ref, lng_ref,
                     lnb_ref, wg_ref, ps_ref, a_ref, c_ref, uext, pext, pooled, ush, *, seq_len):
    i = pl.program_id(1)
    last = pl.num_programs(1) - 1
    keep_prev = (i > 0).astype(F32)
    keep_next = (i < last).astype(F32)
    for ext, cur, prev, nxt in ((uext, u_ref, up_ref, un_ref), (pext, p_ref, pp_ref, pn_ref)):
        ext[0:HALO, :] = prev[...] * keep_prev
        ext[HALO:HALO + TS, :] = cur[...]
        ext[HALO + TS:HALO + TS + HALO, :] = nxt[...] * keep_next

    n_sh = TS + 2 * HALO - SUBLANES
    for r in range(1, SUBLANES):
        ush[r - 1] = uext[r:r + n_sh, :]

    half = CONV_WIDTH // 2
    for r0 in range(0, TS, ROW_CHUNK):
        acc = jnp.zeros((ROW_CHUNK, CONV_CH), F32) + cb_ref[...]
        for kk in range(CONV_WIDTH):
            off = HALO + kk - half
            base = r0 + (off // SUBLANES) * SUBLANES
            if off % SUBLANES:
                rows = ush[off % SUBLANES - 1, base:base + ROW_CHUNK, :]
            else:
                rows = uext[base:base + ROW_CHUNK, :]
            acc = acc + cw_ref[kk:kk + 1, :] * rows
        mu = jnp.mean(acc, axis=-1, keepdims=True)
        d = acc - mu
        var = jnp.mean(d * d, axis=-1, keepdims=True)
        y = d * lax.rsqrt(var + EPS) * lng_ref[...] + lnb_ref[...]
        a_ref[r0:r0 + ROW_CHUNK, :] = (y * jax.nn.sigmoid(y)).astype(BF16)

        t = (i * TS + r0 + lax.broadcasted_iota(jnp.int32, (ROW_CHUNK, 1), 0))
        for gi, w in enumerate(POOL_WINDOWS):
            lanes = slice(gi * POOL_GROUP, (gi + 1) * POOL_GROUP)
            s = jnp.zeros((ROW_CHUNK, POOL_GROUP), F32)
            for j in range(w):
                start = HALO + r0 - w // 2 + j
                s = s + pext[start:start + ROW_CHUNK, lanes]
            cnt = (jnp.minimum(t + w // 2, seq_len) - jnp.maximum(t - w // 2, 0)).astype(F32)
            centre = pext[HALO + r0:HALO + r0 + ROW_CHUNK, lanes]
            pooled[r0:r0 + ROW_CHUNK, lanes] = s / cnt - centre

    mixed = _dot(pooled[...].astype(BF16), wg_ref[...])
    c_ref[...] = (mixed * ps_ref[...]).astype(BF16)


def _convpool(u, p, conv_w, conv_b, ln_g, ln_b, wg_bd, pool_scale):
    bsz, seq, _ = u.shape
    n_halo = seq // HALO
    per = TS // HALO
    cur = lambda: pl.BlockSpec((None, TS, CONV_CH), lambda b, i: (b, i, 0))
    prev = lambda: pl.BlockSpec((None, HALO, CONV_CH),
                                lambda b, i: (b, jnp.maximum(i * per - 1, 0), 0))
    nxt = lambda: pl.BlockSpec((None, HALO, CONV_CH),
                               lambda b, i: (b, jnp.minimum((i + 1) * per, n_halo - 1), 0))
    row = lambda: _const_spec((1, CONV_CH))
    return pl.pallas_call(
        functools.partial(_convpool_kernel, seq_len=seq),
        grid=(bsz, seq // TS),
        in_specs=[cur(), prev(), nxt(), cur(), prev(), nxt(),
                  _const_spec((32, CONV_CH)), row(), row(), row(),
                  _const_spec((POOL_CH, POOL_CH)), row()],
        out_specs=[cur(), cur()],
        out_shape=[jax.ShapeDtypeStruct((bsz, seq, CONV_CH), BF16),
                   jax.ShapeDtypeStruct((bsz, seq, POOL_CH), BF16)],
        scratch_shapes=[pltpu.VMEM((TS + 2 * HALO, CONV_CH), F32),
                        pltpu.VMEM((TS + 2 * HALO, POOL_CH), F32),
                        pltpu.VMEM((TS, POOL_CH), F32),
                        pltpu.VMEM((SUBLANES - 1, TS + 2 * HALO - SUBLANES, CONV_CH), F32)],
        compiler_params=_params(2),
        name="convpool",
    )(u, u, u, p, p, p, conv_w, conv_b, ln_g, ln_b, wg_bd, pool_scale)


def _merge_kernel(x_ref, gt_ref, gpost_ref, a_ref, o_ref, c_ref, gate_ref, wca_ref, wat_ref,
                  wpo_ref, wout_ref, xo_ref):
    ya = _dot(a_ref[...], wca_ref[...])
    o = jnp.concatenate([o_ref[hh] for hh in range(N_HEADS)], axis=1)
    yb = _dot(o, wat_ref[...])
    yc = _dot(c_ref[...], wpo_ref[...])
    m = (gate_ref[:, 0:D_MODEL] * ya + gate_ref[:, D_MODEL:2 * D_MODEL] * yb
         + gate_ref[:, 2 * D_MODEL:3 * D_MODEL] * yc)
    y = _dot(m.astype(BF16), wout_ref[...])
    ms = jnp.mean(y * y, axis=-1, keepdims=True)
    yn = (y * lax.rsqrt(ms + EPS)) * gpost_ref[...]
    xo_ref[...] = x_ref[...] + gt_ref[...] * yn


def _merge(x, mod, layer, row_fn, g_post, a, o, c, gate, wca, wat, wpo, wout, tm):
    bsz, seq, _ = x.shape
    tok = lambda width: pl.BlockSpec((None, tm, width), lambda b, i: (b, i, 0))
    return pl.pallas_call(
        _merge_kernel,
        grid=(bsz, seq // tm),
        in_specs=[tok(D_MODEL), _mod_spec(layer, row_fn, 2), _const_spec((1, D_MODEL)),
                  tok(CONV_CH),
                  pl.BlockSpec((None, N_HEADS, tm, LANES), lambda b, i: (b, 0, i, 0)),
                  tok(POOL_CH), tok(N_BRANCH * D_MODEL),
                  _const_spec((CONV_CH, D_MODEL)), _const_spec((ATTN_W, D_MODEL)),
                  _const_spec((POOL_CH, D_MODEL)), _const_spec((D_MODEL, D_MODEL))],
        out_specs=tok(D_MODEL),
        out_shape=jax.ShapeDtypeStruct(x.shape, F32),
        compiler_params=_params(2),
        name="merge",
    )(x, mod, g_post, a, o, c, gate, wca, wat, wpo, wout)


def _ffn_kernel(x_ref, sc_ref, sh_ref, gt_ref, gpre_ref, gpost_ref, w1_ref, w2_ref, xo_ref):
    x = x_ref[...]
    ms = jnp.mean(x * x, axis=-1, keepdims=True)
    h = (x * lax.rsqrt(ms + EPS)) * gpre_ref[...]
    hb = (h * (1.0 + sc_ref[...]) + sh_ref[...]).astype(BF16)
    y = jnp.zeros(x.shape, F32)
    lo = 0
    for width in FFN_CHUNKS:
        u1 = _dot(hb, w1_ref[:, lo:lo + width])
        u2 = _dot(hb, w1_ref[:, D_FF + lo:D_FF + lo + width])
        act = ((u1 * jax.nn.sigmoid(u1)) * u2).astype(BF16)
        y = y + _dot(act, w2_ref[lo:lo + width, :])
        lo += width
    ms2 = jnp.mean(y * y, axis=-1, keepdims=True)
    yn = (y * lax.rsqrt(ms2 + EPS)) * gpost_ref[...]
    xo_ref[...] = x + gt_ref[...] * yn


def _ffn(x, mod, layer, row_fn, g_pre, g_post, w1, w2, tm):
    bsz, seq, _ = x.shape
    tok = pl.BlockSpec((None, tm, D_MODEL), lambda b, i: (b, i, 0))
    return pl.pallas_call(
        _ffn_kernel,
        grid=(bsz, seq // tm),
        in_specs=[tok, _mod_spec(layer, row_fn, 4), _mod_spec(layer, row_fn, 3),
                  _mod_spec(layer, row_fn, 5), _const_spec((1, D_MODEL)), _const_spec((1, D_MODEL)),
                  _const_spec((D_MODEL, 2 * D_FF)), _const_spec((D_FF, D_MODEL))],
        out_specs=tok,
        out_shape=jax.ShapeDtypeStruct(x.shape, F32),
        compiler_params=_params(2),
        name="ffn",
    )(x, mod, mod, mod, g_pre, g_post, w1, w2)


def _rope_tables(n_tokens):
    rows = n_tokens // GRID_W
    row = jnp.repeat(jnp.arange(rows), GRID_W).astype(F32)
    col = jnp.tile(jnp.arange(GRID_W), rows).astype(F32)
    half = HEAD_DIM // 2
    inv = ROPE_BASE ** (-jnp.arange(0, half, 2, dtype=F32) / half)
    lane = jnp.arange(LANES)
    d = lane % HEAD_DIM
    freq = inv[d % (half // 2)]
    pos = jnp.where((d < half)[None, :], row[:, None], col[:, None])
    ang = pos * freq[None, :]
    first = ((d % half) < half // 2)[None, :]
    sin = jnp.sin(ang)
    return jnp.cos(ang), jnp.where(first, -sin, 0.0), jnp.where(first, 0.0, sin)


def kernel(x, c, ctx, c_ctx, w_mod, b_mod, g_pre_mix, g_post_mix, w_in, b_gate, conv_w, conv_b, conv_ln_g, conv_ln_b, w_conv_out, lam_q1, lam_k1, lam_q2, lam_k2, subln_g, w_attn_out, w_pool_group, pool_scale, w_pool_out, w_out, g_pre_ffn, g_post_ffn, w_ffn_in, w_ffn_out):
    bsz, seq, _ = x.shape
    depth = w_mod.shape[0]
    assert ctx.shape[1] == CTX_LEN and seq % TM_LAT == 0 and seq % GRID_W == 0
    assert bsz + 1 <= MOD_ROWS

    c_all = jnp.zeros((MOD_ROWS, D_MODEL), F32).at[:bsz].set(c).at[bsz].set(c_ctx)
    mod = _modulation(c_all, w_mod, b_mod).reshape(depth, MOD_ROWS, 1, N_MOD * D_MODEL)
    lat_row = lambda b: b
    ctx_row = lambda b: bsz

    rope_lat = _rope_tables(seq)
    rope_ctx = (jnp.ones((CTX_LEN, LANES), F32), jnp.zeros((CTX_LEN, LANES), F32),
                jnp.zeros((CTX_LEN, LANES), F32))
    row2 = lambda v: v.reshape(1, -1)

    xl, xc = x, ctx
    for l in range(depth):
        need_ctx = l < depth - 1
        lam_init = 0.8 - 0.6 * math.exp(-0.3 * l)
        wl = w_in[l]
        w_rest = jnp.concatenate([wl[:, :COL_V], wl[:, COL_P:]], axis=1).astype(BF16)
        wvt = wl[:, COL_V:COL_P].T.astype(BF16)
        bg = row2(b_gate[l])
        lamp = jnp.stack([lam_q1[l], lam_k1[l], lam_q2[l], lam_k2[l]])
        gs = subln_g[l].reshape(V_DIM, 1)
        cw = jnp.zeros((32, CONV_CH), F32).at[:CONV_WIDTH].set(conv_w[l])
        wg_bd = jax.scipy.linalg.block_diag(*[w_pool_group[l, g] for g in range(len(POOL_WINDOWS))]).astype(BF16)
        wca, wat = w_conv_out[l].astype(BF16), w_attn_out[l].astype(BF16)
        wpo, wout = w_pool_out[l].astype(BF16), w_out[l].astype(BF16)
        w1, w2 = w_ffn_in[l].astype(BF16), w_ffn_out[l].astype(BF16)
        gpm, gqm = row2(g_pre_mix[l]), row2(g_post_mix[l])
        gpf, gqf = row2(g_pre_ffn[l]), row2(g_post_ffn[l])
        conv_args = (cw, row2(conv_b[l]), row2(conv_ln_g[l]), row2(conv_ln_b[l]), wg_bd,
                     row2(pool_scale[l]))

        uc, qc, kc, vtc, pc, gc = _inproj(xc, mod, l, ctx_row, gpm, w_rest, wvt, bg, rope_ctx,
                                          CTX_LEN, CTX_LEN)
        ul, ql, kl, vtl, pl_, gl = _inproj(xl, mod, l, lat_row, gpm, w_rest, wvt, bg, rope_lat,
                                           TM_LAT, TK)
        o_l = _attention(ql, kc, vtc, kl, vtl, lamp, gs, lam_init)
        a_l, c_l = _convpool(ul, pl_, *conv_args)
        xl = _merge(xl, mod, l, lat_row, gqm, a_l, o_l, c_l, gl, wca, wat, wpo, wout, TM_LAT)
        xl = _ffn(xl, mod, l, lat_row, gpf, gqf, w1, w2, TM_LAT)
        if need_ctx:
            o_c = _attention(qc, kc, vtc, None, None, lamp, gs, lam_init)
            a_c, c_c = _convpool(uc, pc, *conv_args)
            xc = _merge(xc, mod, l, ctx_row, gqm, a_c, o_c, c_c, gc, wca, wat, wpo, wout, CTX_LEN)
            xc = _ffn(xc, mod, l, ctx_row, gpf, gqf, w1, w2, CTX_LEN)
    return xl
```

```python
import functools
import math

import jax
import jax.numpy as jnp
from jax import lax
from jax.experimental import pallas as pl
from jax.experimental.pallas import tpu as pltpu

F32 = jnp.float32
BF16 = jnp.bfloat16

D_MODEL = 1024
DEPTH = 4
GRID_W = 64
CTX_LEN = 256
EPS = 1e-6
CONV_CH = 512
CONV_WIDTH = 31
N_HEADS = 8
HEAD_DIM = 64
V_DIM = 2 * HEAD_DIM
QK_W = N_HEADS * 2 * HEAD_DIM
ATTN_W = N_HEADS * V_DIM
ROPE_BASE = 10000.0
POOL_CH = 512
POOL_WINDOWS = (2, 4, 8, 16)
POOL_GROUP = POOL_CH // len(POOL_WINDOWS)
N_BRANCH = 3
D_FF = -(-(8 * D_MODEL) // (3 * 256)) * 256
N_MOD = 6
COL_A = 0
COL_Q = COL_A + 2 * CONV_CH
COL_K = COL_Q + QK_W
COL_V = COL_K + QK_W
COL_P = COL_V + ATTN_W
COL_G = COL_P + POOL_CH
IN_W = COL_G + N_BRANCH * D_MODEL

LANES = 128
SUBLANES = 8
HALO = 16
MOD_ROWS = 24
VMEM_LIMIT = 56 * 1024 * 1024

TM_LAT = 512
TQ = 1024
TK = 512
KEY_GROUP = 1
LOOKAHEAD = 2
TS = 256
ROW_CHUNK = 32
FFN_CHUNKS = (1024, 1024, 768)
NEG_BIG = -1e30
VT_ROWS = V_DIM + 16
LOG2E = math.log2(math.e)


def _dot(a, b):
    return jnp.dot(a, b, preferred_element_type=F32)


def _dot_nt(a, b):
    return lax.dot_general(a, b, (((1,), (1,)), ((), ())), preferred_element_type=F32)


def _const_spec(shape):
    nd = len(shape)
    return pl.BlockSpec(shape, lambda *_: (0,) * nd, pipeline_mode=pl.Buffered(1))


def _params(n_axes):
    return pltpu.CompilerParams(dimension_semantics=("arbitrary",) * n_axes,
                                vmem_limit_bytes=VMEM_LIMIT)


def _mod_spec(layer, row_fn, chunk):
    return pl.BlockSpec((None, None, 1, D_MODEL),
                        lambda b, *_: (layer, row_fn(b), 0, chunk))


def _mod_kernel(c_ref, w_ref, b_ref, o_ref):
    c = c_ref[...]
    a = (c * jax.nn.sigmoid(c)).astype(BF16)
    o_ref[...] = _dot(a, w_ref[...].astype(BF16)) + b_ref[...]


def _modulation(c_all, w_mod, b_mod):
    depth = w_mod.shape[0]
    tn = 1536
    return pl.pallas_call(
        _mod_kernel,
        grid=(depth, (N_MOD * D_MODEL) // tn),
        in_specs=[pl.BlockSpec((MOD_ROWS, D_MODEL), lambda l, j: (0, 0)),
                  pl.BlockSpec((None, D_MODEL, tn), lambda l, j: (l, 0, j)),
                  pl.BlockSpec((None, 1, tn), lambda l, j: (l, 0, j))],
        out_specs=pl.BlockSpec((None, MOD_ROWS, tn), lambda l, j: (l, 0, j)),
        out_shape=jax.ShapeDtypeStruct((depth, MOD_ROWS, N_MOD * D_MODEL), F32),
        compiler_params=_params(2),
        name="modulation",
    )(c_all, w_mod, b_mod.reshape(depth, 1, N_MOD * D_MODEL))


def _inproj_kernel(x_ref, sc_ref, sh_ref, g_ref, w_ref, wvt_ref, bg_ref, cos_ref, sa_ref, sb_ref,
                   u_ref, q_ref, k_ref, vt_ref, p_ref, gate_ref, *, tk):
    x = x_ref[...]
    ms = jnp.mean(x * x, axis=-1, keepdims=True)
    h = (x * lax.rsqrt(ms + EPS)) * g_ref[...]
    h = h * (1.0 + sc_ref[...]) + sh_ref[...]
    hb = h.astype(BF16)
    tm = x.shape[0]

    a = _dot(hb, w_ref[:, 0:2 * CONV_CH])
    u_ref[...] = a[:, :CONV_CH] * jax.nn.sigmoid(a[:, CONV_CH:])

    cos = cos_ref[...]
    sa = sa_ref[...]
    sb = sb_ref[...]
    for col, out_ref, scale in ((2 * CONV_CH, q_ref, HEAD_DIM ** -0.5 * LOG2E),
                                (2 * CONV_CH + QK_W, k_ref, None)):
        z = _dot(hb, w_ref[:, col:col + QK_W])
        for hh in range(N_HEADS):
            zh = z[:, hh * LANES:(hh + 1) * LANES]
            r = zh * cos + pltpu.roll(zh, LANES - 16, 1) * sa + pltpu.roll(zh, 16, 1) * sb
            if scale is not None:
                r = r * scale
            out_ref[hh] = r.astype(BF16)

    vt = _dot_nt(wvt_ref[...], hb)
    ones_rows = (lax.broadcasted_iota(jnp.int32, (VT_ROWS - V_DIM, tk), 0) == 0).astype(BF16)
    for hh in range(N_HEADS):
        for c in range(tm // tk):
            vt_ref[hh, c, 0:V_DIM, :] = vt[hh * V_DIM:(hh + 1) * V_DIM,
                                           c * tk:(c + 1) * tk].astype(BF16)
            vt_ref[hh, c, V_DIM:VT_ROWS, :] = ones_rows

    col_p = 2 * CONV_CH + 2 * QK_W
    p_ref[...] = _dot(hb, w_ref[:, col_p:col_p + POOL_CH])
    col_g = col_p + POOL_CH
    for j in range(N_BRANCH):
        lo = col_g + j * D_MODEL
        gl = _dot(hb, w_ref[:, lo:lo + D_MODEL]) + bg_ref[:, j * D_MODEL:(j + 1) * D_MODEL]
        gate_ref[:, j * D_MODEL:(j + 1) * D_MODEL] = jax.nn.sigmoid(gl).astype(BF16)


def _inproj(x, mod, layer, row_fn, g_pre, w_rest, wvt, b_gate, rope, tm, tk):
    bsz, seq, _ = x.shape
    cos, sa, sb = rope
    n_w = w_rest.shape[1]
    tok = lambda width: pl.BlockSpec((None, tm, width), lambda b, i: (b, i, 0))
    head = pl.BlockSpec((None, N_HEADS, tm, LANES), lambda b, i: (b, 0, i, 0))
    tab = pl.BlockSpec((tm, LANES), lambda b, i: (i, 0))
    return pl.pallas_call(
        functools.partial(_inproj_kernel, tk=tk),
        grid=(bsz, seq // tm),
        in_specs=[tok(D_MODEL), _mod_spec(layer, row_fn, 1), _mod_spec(layer, row_fn, 0),
                  _const_spec((1, D_MODEL)), _const_spec((D_MODEL, n_w)),
                  _const_spec((ATTN_W, D_MODEL)), _const_spec((1, N_BRANCH * D_MODEL)),
                  tab, tab, tab],
        out_specs=[tok(CONV_CH), head, head,
                   pl.BlockSpec((None, N_HEADS, tm // tk, VT_ROWS, tk), lambda b, i: (b, 0, i, 0, 0)),
                   tok(POOL_CH), tok(N_BRANCH * D_MODEL)],
        out_shape=[jax.ShapeDtypeStruct((bsz, seq, CONV_CH), F32),
                   jax.ShapeDtypeStruct((bsz, N_HEADS, seq, LANES), BF16),
                   jax.ShapeDtypeStruct((bsz, N_HEADS, seq, LANES), BF16),
                   jax.ShapeDtypeStruct((bsz, N_HEADS, seq // tk, VT_ROWS, tk), BF16),
                   jax.ShapeDtypeStruct((bsz, seq, POOL_CH), F32),
                   jax.ShapeDtypeStruct((bsz, seq, N_BRANCH * D_MODEL), BF16)],
        compiler_params=_params(2),
        name="inproj",
    )(x, mod, mod, g_pre, w_rest, wvt, b_gate, cos, sa, sb)


def _attn_kernel(*refs, n_lat, lam_init):
    if n_lat:
        lamp_ref, gs_ref, q_ref, kc_ref, vtc_ref, k_ref, vt_ref, o_ref = refs
    else:
        lamp_ref, gs_ref, q_ref, kc_ref, vtc_ref, o_ref = refs
    tq = q_ref.shape[0]
    q = q_ref[...].astype(F32)
    lane = lax.broadcasted_iota(jnp.int32, q.shape, 1)
    first = (lane < HEAD_DIM).astype(F32)
    qs = jnp.concatenate([q * first, q * (1.0 - first)], axis=0).astype(BF16)

    def scores(keys):
        return [_dot_nt(kch, qs) for kch in keys]

    steps = [([kc_ref], [vtc_ref])]
    if n_lat:
        tk = k_ref.shape[0] // n_lat
        for j0 in range(0, n_lat, KEY_GROUP):
            js = range(j0, j0 + KEY_GROUP)
            steps.append(([k_ref.at[j * tk:(j + 1) * tk, :] for j in js],
                          [vt_ref.at[j] for j in js]))

    def pipelined(lookahead, consume, state):
        queue = [scores([r[...] for r in steps[i][0]]) for i in range(min(lookahead, len(steps)))]
        for i, (_, vts) in enumerate(steps):
            if i + lookahead < len(steps):
                queue.append(scores([r[...] for r in steps[i + lookahead][0]]))
            state = consume(state, queue.pop(0), [r[...] for r in vts])
        return state

    def online_step(state, ss, vts):
        m, acc = state
        m_new = m
        for s in ss:
            m_new = jnp.maximum(m_new, jnp.max(s, axis=0, keepdims=True))
        acc_new = jnp.exp2(m - m_new) * acc
        for s, vtch in zip(ss, vts):
            acc_new = acc_new + _dot(vtch, jnp.exp2(s - m_new).astype(BF16))
        return m_new, acc_new

    def exact():
        init = (jnp.full((1, 2 * tq), NEG_BIG, F32), jnp.zeros((VT_ROWS, 2 * tq), F32))
        return pipelined(LOOKAHEAD, online_step, init)[1]

    def fixed_step(state, ss, vts):
        m_ref, acc = state
        if m_ref is None:
            m_ref = jnp.max(ss[0], axis=0, keepdims=True)
        for s, vtch in zip(ss, vts):
            acc = acc + _dot(vtch, jnp.exp2(s - m_ref).astype(BF16))
        return m_ref, acc

    def optimistic():
        return pipelined(LOOKAHEAD, fixed_step,
                         (None, jnp.zeros((VT_ROWS, 2 * tq), F32)))[1]

    def finish(acc):
        lp = lamp_ref[...]
        lam = (jnp.exp(jnp.sum(lp[0:1] * lp[1:2], axis=1, keepdims=True))
               - jnp.exp(jnp.sum(lp[2:3] * lp[3:4], axis=1, keepdims=True)) + lam_init)
        on = acc[0:V_DIM] * (1.0 / acc[V_DIM:V_DIM + 1])
        o = on[:, :tq] - lam * on[:, tq:]
        ms = jnp.mean(o * o, axis=0, keepdims=True)
        o = (o * lax.rsqrt(ms + EPS)) * gs_ref[...]
        o = o * (1.0 - lam_init)
        o_ref[...] = o.T.astype(BF16)

    if n_lat:
        acc = optimistic()
        finish(acc)
        overflowed = jnp.logical_not(jnp.sum(acc * 0.0) == 0.0)

        @pl.when(overflowed)
        def _():
            finish(exact())
    else:
        finish(exact())


def _attention(q, kc, vtc, k, vt, lamp, gs, lam_init):
    bsz, _, sq, _ = q.shape
    n_lat = 0 if k is None else vt.shape[2]
    tq = min(TQ, sq)
    qspec = pl.BlockSpec((None, None, tq, LANES), lambda b, h, i: (b, h, i, 0))
    in_specs = [_const_spec((4, HEAD_DIM)), _const_spec((V_DIM, 1)), qspec,
                pl.BlockSpec((None, None, CTX_LEN, LANES), lambda b, h, i: (b, h, 0, 0)),
                pl.BlockSpec((None, None, None, VT_ROWS, CTX_LEN), lambda b, h, i: (b, h, 0, 0, 0))]
    args = [lamp, gs, q, kc, vtc]
    if n_lat:
        in_specs += [pl.BlockSpec((None, None, k.shape[2], LANES), lambda b, h, i: (b, h, 0, 0)),
                     pl.BlockSpec((None, None, n_lat, VT_ROWS, vt.shape[4]),
                                  lambda b, h, i: (b, h, 0, 0, 0))]
        args += [k, vt]
    return pl.pallas_call(
        functools.partial(_attn_kernel, n_lat=n_lat, lam_init=lam_init),
        grid=(bsz, N_HEADS, sq // tq),
        in_specs=in_specs,
        out_specs=qspec,
        out_shape=jax.ShapeDtypeStruct(q.shape, BF16),
        compiler_params=_params(3),
        name="attn_lat" if n_lat else "attn_ctx",
    )(*args)


def _convpool_kernel(u_ref, up_ref, un_ref, p_ref, pp_ref, pn_ref, cw_ref, cb_ref, lng_ref,
                     lnb_ref, wg_ref, ps_ref, a_ref, c_ref, uext, pext, pooled, ush, *, seq_len):
    i = pl.program_id(1)
    last = pl.num_programs(1) - 1
    keep_prev = (i > 0).astype(F32)
    keep_next = (i < last).astype(F32)
    for ext, cur, prev, nxt in ((uext, u_ref, up_ref, un_ref), (pext, p_ref, pp_ref, pn_ref)):
        ext[0:HALO, :] = prev[...] * keep_prev
        ext[HALO:HALO + TS, :] = cur[...]
        ext[HALO + TS:HALO + TS + HALO, :] = nxt[...] * keep_next

    n_sh = TS + 2 * HALO - SUBLANES
    for r in range(1, SUBLANES):
        ush[r - 1] = uext[r:r + n_sh, :]

    half = CONV_WIDTH // 2
    for r0 in range(0, TS, ROW_CHUNK):
        acc = jnp.zeros((ROW_CHUNK, CONV_CH), F32) + cb_ref[...]
        for kk in range(CONV_WIDTH):
            off = HALO + kk - half
            base = r0 + (off // SUBLANES) * SUBLANES
            if off % SUBLANES:
                rows = ush[off % SUBLANES - 1, base:base + ROW_CHUNK, :]
            else:
                rows = uext[base:base + ROW_CHUNK, :]
            acc = acc + cw_ref[kk:kk + 1, :] * rows
        mu = jnp.mean(acc, axis=-1, keepdims=True)
        d = acc - mu
        var = jnp.mean(d * d, axis=-1, keepdims=True)
        y = d * lax.rsqrt(var + EPS) * lng_ref[...] + lnb_ref[...]
        a_ref[r0:r0 + ROW_CHUNK, :] = (y * jax.nn.sigmoid(y)).astype(BF16)

        t = (i * TS + r0 + lax.broadcasted_iota(jnp.int32, (ROW_CHUNK, 1), 0))
        for gi, w in enumerate(POOL_WINDOWS):
            lanes = slice(gi * POOL_GROUP, (gi + 1) * POOL_GROUP)
            s = jnp.zeros((ROW_CHUNK, POOL_GROUP), F32)
            for j in range(w):
                start = HALO + r0 - w // 2 + j
                s = s + pext[start:start + ROW_CHUNK, lanes]
            cnt = (jnp.minimum(t + w // 2, seq_len) - jnp.maximum(t - w // 2, 0)).astype(F32)
            centre = pext[HALO + r0:HALO + r0 + ROW_CHUNK, lanes]
            pooled[r0:r0 + ROW_CHUNK, lanes] = s / cnt - centre

    mixed = _dot(pooled[...].astype(BF16), wg_ref[...])
    c_ref[...] = (mixed * ps_ref[...]).astype(BF16)


def _convpool(u, p, conv_w, conv_b, ln_g, ln_b, wg_bd, pool_scale):
    bsz, seq, _ = u.shape
    n_halo = seq // HALO
    per = TS // HALO
    cur = lambda: pl.BlockSpec((None, TS, CONV_CH), lambda b, i: (b, i, 0))
    prev = lambda: pl.BlockSpec((None, HALO, CONV_CH),
                                lambda b, i: (b, jnp.maximum(i * per - 1, 0), 0))
    nxt = lambda: pl.BlockSpec((None, HALO, CONV_CH),
                               lambda b, i: (b, jnp.minimum((i + 1) * per, n_halo - 1), 0))
    row = lambda: _const_spec((1, CONV_CH))
    return pl.pallas_call(
        functools.partial(_convpool_kernel, seq_len=seq),
        grid=(bsz, seq // TS),
        in_specs=[cur(), prev(), nxt(), cur(), prev(), nxt(),
                  _const_spec((32, CONV_CH)), row(), row(), row(),
                  _const_spec((POOL_CH, POOL_CH)), row()],
        out_specs=[cur(), cur()],
        out_shape=[jax.ShapeDtypeStruct((bsz, seq, CONV_CH), BF16),
                   jax.ShapeDtypeStruct((bsz, seq, POOL_CH), BF16)],
        scratch_shapes=[pltpu.VMEM((TS + 2 * HALO, CONV_CH), F32),
                        pltpu.VMEM((TS + 2 * HALO, POOL_CH), F32),
                        pltpu.VMEM((TS, POOL_CH), F32),
                        pltpu.VMEM((SUBLANES - 1, TS + 2 * HALO - SUBLANES, CONV_CH), F32)],
        compiler_params=_params(2),
        name="convpool",
    )(u, u, u, p, p, p, conv_w, conv_b, ln_g, ln_b, wg_bd, pool_scale)


def _merge_kernel(x_ref, gt_ref, gpost_ref, a_ref, o_ref, c_ref, gate_ref, wca_ref, wat_ref,
                  wpo_ref, wout_ref, xo_ref):
    ya = _dot(a_ref[...], wca_ref[...])
    o = jnp.concatenate([o_ref[hh] for hh in range(N_HEADS)], axis=1)
    yb = _dot(o, wat_ref[...])
    yc = _dot(c_ref[...], wpo_ref[...])
    m = (gate_ref[:, 0:D_MODEL] * ya + gate_ref[:, D_MODEL:2 * D_MODEL] * yb
         + gate_ref[:, 2 * D_MODEL:3 * D_MODEL] * yc)
    y = _dot(m.astype(BF16), wout_ref[...])
    ms = jnp.mean(y * y, axis=-1, keepdims=True)
    yn = (y * lax.rsqrt(ms + EPS)) * gpost_ref[...]
    xo_ref[...] = x_ref[...] + gt_ref[...] * yn


def _merge(x, mod, layer, row_fn, g_post, a, o, c, gate, wca, wat, wpo, wout, tm):
    bsz, seq, _ = x.shape
    tok = lambda width: pl.BlockSpec((None, tm, width), lambda b, i: (b, i, 0))
    return pl.pallas_call(
        _merge_kernel,
        grid=(bsz, seq // tm),
        in_specs=[tok(D_MODEL), _mod_spec(layer, row_fn, 2), _const_spec((1, D_MODEL)),
                  tok(CONV_CH),
                  pl.BlockSpec((None, N_HEADS, tm, LANES), lambda b, i: (b, 0, i, 0)),
                  tok(POOL_CH), tok(N_BRANCH * D_MODEL),
                  _const_spec((CONV_CH, D_MODEL)), _const_spec((ATTN_W, D_MODEL)),
                  _const_spec((POOL_CH, D_MODEL)), _const_spec((D_MODEL, D_MODEL))],
        out_specs=tok(D_MODEL),
        out_shape=jax.ShapeDtypeStruct(x.shape, F32),
        compiler_params=_params(2),
        name="merge",
    )(x, mod, g_post, a, o, c, gate, wca, wat, wpo, wout)


def _ffn_kernel(x_ref, sc_ref, sh_ref, gt_ref, gpre_ref, gpost_ref, w1_ref, w2_ref, xo_ref):
    x = x_ref[...]
    ms = jnp.mean(x * x, axis=-1, keepdims=True)
    h = (x * lax.rsqrt(ms + EPS)) * gpre_ref[...]
    hb = (h * (1.0 + sc_ref[...]) + sh_ref[...]).astype(BF16)
    y = jnp.zeros(x.shape, F32)
    lo = 0
    for width in FFN_CHUNKS:
        u1 = _dot(hb, w1_ref[:, lo:lo + width])
        u2 = _dot(hb, w1_ref[:, D_FF + lo:D_FF + lo + width])
        act = ((u1 * jax.nn.sigmoid(u1)) * u2).astype(BF16)
        y = y + _dot(act, w2_ref[lo:lo + width, :])
        lo += width
    ms2 = jnp.mean(y * y, axis=-1, keepdims=True)
    yn = (y * lax.rsqrt(ms2 + EPS)) * gpost_ref[...]
    xo_ref[...] = x + gt_ref[...] * yn


def _ffn(x, mod, layer, row_fn, g_pre, g_post, w1, w2, tm):
    bsz, seq, _ = x.shape
    tok = pl.BlockSpec((None, tm, D_MODEL), lambda b, i: (b, i, 0))
    return pl.pallas_call(
        _ffn_kernel,
        grid=(bsz, seq // tm),
        in_specs=[tok, _mod_spec(layer, row_fn, 4), _mod_spec(layer, row_fn, 3),
                  _mod_spec(layer, row_fn, 5), _const_spec((1, D_MODEL)), _const_spec((1, D_MODEL)),
                  _const_spec((D_MODEL, 2 * D_FF)), _const_spec((D_FF, D_MODEL))],
        out_specs=tok,
        out_shape=jax.ShapeDtypeStruct(x.shape, F32),
        compiler_params=_params(2),
        name="ffn",
    )(x, mod, mod, mod, g_pre, g_post, w1, w2)


def _rope_tables(n_tokens):
    rows = n_tokens // GRID_W
    row = jnp.repeat(jnp.arange(rows), GRID_W).astype(F32)
    col = jnp.tile(jnp.arange(GRID_W), rows).astype(F32)
    half = HEAD_DIM // 2
    inv = ROPE_BASE ** (-jnp.arange(0, half, 2, dtype=F32) / half)
    lane = jnp.arange(LANES)
    d = lane % HEAD_DIM
    freq = inv[d % (half // 2)]
    pos = jnp.where((d < half)[None, :], row[:, None], col[:, None])
    ang = pos * freq[None, :]
    first = ((d % half) < half // 2)[None, :]
    sin = jnp.sin(ang)
    return jnp.cos(ang), jnp.where(first, -sin, 0.0), jnp.where(first, 0.0, sin)


def kernel(x, c, ctx, c_ctx, w_mod, b_mod, g_pre_mix, g_post_mix, w_in, b_gate, conv_w, conv_b, conv_ln_g, conv_ln_b, w_conv_out, lam_q1, lam_k1, lam_q2, lam_k2, subln_g, w_attn_out, w_pool_group, pool_scale, w_pool_out, w_out, g_pre_ffn, g_post_ffn, w_ffn_in, w_ffn_out):
    bsz, seq, _ = x.shape
    depth = w_mod.shape[0]
    assert ctx.shape[1] == CTX_LEN and seq % TM_LAT == 0 and seq % GRID_W == 0
    assert bsz + 1 <= MOD_ROWS

    c_all = jnp.zeros((MOD_ROWS, D_MODEL), F32).at[:bsz].set(c).at[bsz].set(c_ctx)
    mod = _modulation(c_all, w_mod, b_mod).reshape(depth, MOD_ROWS, 1, N_MOD * D_MODEL)
    lat_row = lambda b: b
    ctx_row = lambda b: bsz

    rope_lat = _rope_tables(seq)
    rope_ctx = (jnp.ones((CTX_LEN, LANES), F32), jnp.zeros((CTX_LEN, LANES), F32),
                jnp.zeros((CTX_LEN, LANES), F32))
    row2 = lambda v: v.reshape(1, -1)

    xl, xc = x, ctx
    for l in range(depth):
        need_ctx = l < depth - 1
        lam_init = 0.8 - 0.6 * math.exp(-0.3 * l)
        wl = w_in[l]
        w_rest = jnp.concatenate([wl[:, :COL_V], wl[:, COL_P:]], axis=1).astype(BF16)
        wvt = wl[:, COL_V:COL_P].T.astype(BF16)
        bg = row2(b_gate[l])
        lamp = jnp.stack([lam_q1[l], lam_k1[l], lam_q2[l], lam_k2[l]])
        gs = subln_g[l].reshape(V_DIM, 1)
        cw = jnp.zeros((32, CONV_CH), F32).at[:CONV_WIDTH].set(conv_w[l])
        wg_bd = jax.scipy.linalg.block_diag(*[w_pool_group[l, g] for g in range(len(POOL_WINDOWS))]).astype(BF16)
        wca, wat = w_conv_out[l].astype(BF16), w_attn_out[l].astype(BF16)
        wpo, wout = w_pool_out[l].astype(BF16), w_out[l].astype(BF16)
        w1, w2 = w_ffn_in[l].astype(BF16), w_ffn_out[l].astype(BF16)
        gpm, gqm = row2(g_pre_mix[l]), row2(g_post_mix[l])
        gpf, gqf = row2(g_pre_ffn[l]), row2(g_post_ffn[l])
        conv_args = (cw, row2(conv_b[l]), row2(conv_ln_g[l]), row2(conv_ln_b[l]), wg_bd,
                     row2(pool_scale[l]))

        uc, qc, kc, vtc, pc, gc = _inproj(xc, mod, l, ctx_row, gpm, w_rest, wvt, bg, rope_ctx,
                                          CTX_LEN, CTX_LEN)
        ul, ql, kl, vtl, pl_, gl = _inproj(xl, mod, l, lat_row, gpm, w_rest, wvt, bg, rope_lat,
                                           TM_LAT, TK)
        o_l = _attention(ql, kc, vtc, kl, vtl, lamp, gs, lam_init)
        a_l, c_l = _convpool(ul, pl_, *conv_args)
        xl = _merge(xl, mod, l, lat_row, gqm, a_l, o_l, c_l, gl, wca, wat, wpo, wout, TM_LAT)
        xl = _ffn(xl, mod, l, lat_row, gpf, gqf, w1, w2, TM_LAT)
        if need_ctx:
            o_c = _attention(qc, kc, vtc, None, None, lamp, gs, lam_init)
            a_c, c_c = _convpool(uc, pc, *conv_args)
            xc = _merge(xc, mod, l, ctx_row, gqm, a_c, o_c, c_c, gc, wca, wat, wpo, wout, CTX_LEN)
            xc = _ffn(xc, mod, l, ctx_row, gpf, gqf, w1, w2, CTX_LEN)
    return xl
```

```python
import functools
import math

import jax
import jax.numpy as jnp
from jax import lax
from jax.experimental import pallas as pl
from jax.experimental.pallas import tpu as pltpu

F32 = jnp.float32
BF16 = jnp.bfloat16

D_MODEL = 1024
DEPTH = 4
GRID_W = 64
CTX_LEN = 256
EPS = 1e-6
CONV_CH = 512
CONV_WIDTH = 31
N_HEADS = 8
HEAD_DIM = 64
V_DIM = 2 * HEAD_DIM
QK_W = N_HEADS * 2 * HEAD_DIM
ATTN_W = N_HEADS * V_DIM
ROPE_BASE = 10000.0
POOL_CH = 512
POOL_WINDOWS = (2, 4, 8, 16)
POOL_GROUP = POOL_CH // len(POOL_WINDOWS)
N_BRANCH = 3
D_FF = -(-(8 * D_MODEL) // (3 * 256)) * 256
N_MOD = 6
COL_A = 0
COL_Q = COL_A + 2 * CONV_CH
COL_K = COL_Q + QK_W
COL_V = COL_K + QK_W
COL_P = COL_V + ATTN_W
COL_G = COL_P + POOL_CH
IN_W = COL_G + N_BRANCH * D_MODEL

LANES = 128
SUBLANES = 8
HALO = 16
MOD_ROWS = 24
VMEM_LIMIT = 56 * 1024 * 1024

TM_LAT = 512
TQ = 1024
TK = 512
KEY_GROUP = 1
LOOKAHEAD = 2
TS = 256
ROW_CHUNK = 64
FFN_CHUNKS = (1024, 1024, 768)
NEG_BIG = -1e30
VT_ROWS = V_DIM + 16
LOG2E = math.log2(math.e)


def _dot(a, b):
    return jnp.dot(a, b, preferred_element_type=F32)


def _dot_nt(a, b):
    return lax.dot_general(a, b, (((1,), (1,)), ((), ())), preferred_element_type=F32)


def _const_spec(shape):
    nd = len(shape)
    return pl.BlockSpec(shape, lambda *_: (0,) * nd, pipeline_mode=pl.Buffered(1))


def _params(n_axes):
    return pltpu.CompilerParams(dimension_semantics=("arbitrary",) * n_axes,
                                vmem_limit_bytes=VMEM_LIMIT)


def _mod_spec(layer, row_fn, chunk):
    return pl.BlockSpec((None, None, 1, D_MODEL),
                        lambda b, *_: (layer, row_fn(b), 0, chunk))


def _mod_kernel(c_ref, w_ref, b_ref, o_ref):
    c = c_ref[...]
    a = (c * jax.nn.sigmoid(c)).astype(BF16)
    o_ref[...] = _dot(a, w_ref[...].astype(BF16)) + b_ref[...]


def _modulation(c_all, w_mod, b_mod):
    depth = w_mod.shape[0]
    tn = 1536
    return pl.pallas_call(
        _mod_kernel,
        grid=(depth, (N_MOD * D_MODEL) // tn),
        in_specs=[pl.BlockSpec((MOD_ROWS, D_MODEL), lambda l, j: (0, 0)),
                  pl.BlockSpec((None, D_MODEL, tn), lambda l, j: (l, 0, j)),
                  pl.BlockSpec((None, 1, tn), lambda l, j: (l, 0, j))],
        out_specs=pl.BlockSpec((None, MOD_ROWS, tn), lambda l, j: (l, 0, j)),
        out_shape=jax.ShapeDtypeStruct((depth, MOD_ROWS, N_MOD * D_MODEL), F32),
        compiler_params=_params(2),
        name="modulation",
    )(c_all, w_mod, b_mod.reshape(depth, 1, N_MOD * D_MODEL))


def _inproj_kernel(x_ref, sc_ref, sh_ref, g_ref, w_ref, wvt_ref, bg_ref, cos_ref, sa_ref, sb_ref,
                   u_ref, q_ref, k_ref, vt_ref, p_ref, gate_ref, *, tk):
    x = x_ref[...]
    ms = jnp.mean(x * x, axis=-1, keepdims=True)
    h = (x * lax.rsqrt(ms + EPS)) * g_ref[...]
    h = h * (1.0 + sc_ref[...]) + sh_ref[...]
    hb = h.astype(BF16)
    tm = x.shape[0]

    a = _dot(hb, w_ref[:, 0:2 * CONV_CH])
    u_ref[...] = a[:, :CONV_CH] * jax.nn.sigmoid(a[:, CONV_CH:])

    cos = cos_ref[...]
    sa = sa_ref[...]
    sb = sb_ref[...]
    for col, out_ref, scale in ((2 * CONV_CH, q_ref, HEAD_DIM ** -0.5 * LOG2E),
                                (2 * CONV_CH + QK_W, k_ref, None)):
        z = _dot(hb, w_ref[:, col:col + QK_W])
        for hh in range(N_HEADS):
            zh = z[:, hh * LANES:(hh + 1) * LANES]
            r = zh * cos + pltpu.roll(zh, LANES - 16, 1) * sa + pltpu.roll(zh, 16, 1) * sb
            if scale is not None:
                r = r * scale
            out_ref[hh] = r.astype(BF16)

    vt = _dot_nt(wvt_ref[...], hb)
    ones_rows = (lax.broadcasted_iota(jnp.int32, (VT_ROWS - V_DIM, tk), 0) == 0).astype(BF16)
    for hh in range(N_HEADS):
        for c in range(tm // tk):
            vt_ref[hh, c, 0:V_DIM, :] = vt[hh * V_DIM:(hh + 1) * V_DIM,
                                           c * tk:(c + 1) * tk].astype(BF16)
            vt_ref[hh, c, V_DIM:VT_ROWS, :] = ones_rows

    col_p = 2 * CONV_CH + 2 * QK_W
    p_ref[...] = _dot(hb, w_ref[:, col_p:col_p + POOL_CH])
    col_g = col_p + POOL_CH
    for j in range(N_BRANCH):
        lo = col_g + j * D_MODEL
        gl = _dot(hb, w_ref[:, lo:lo + D_MODEL]) + bg_ref[:, j * D_MODEL:(j + 1) * D_MODEL]
        gate_ref[:, j * D_MODEL:(j + 1) * D_MODEL] = jax.nn.sigmoid(gl).astype(BF16)


def _inproj(x, mod, layer, row_fn, g_pre, w_rest, wvt, b_gate, rope, tm, tk):
    bsz, seq, _ = x.shape
    cos, sa, sb = rope
    n_w = w_rest.shape[1]
    tok = lambda width: pl.BlockSpec((None, tm, width), lambda b, i: (b, i, 0))
    head = pl.BlockSpec((None, N_HEADS, tm, LANES), lambda b, i: (b, 0, i, 0))
    tab = pl.BlockSpec((tm, LANES), lambda b, i: (i, 0))
    return pl.pallas_call(
        functools.partial(_inproj_kernel, tk=tk),
        grid=(bsz, seq // tm),
        in_specs=[tok(D_MODEL), _mod_spec(layer, row_fn, 1), _mod_spec(layer, row_fn, 0),
                  _const_spec((1, D_MODEL)), _const_spec((D_MODEL, n_w)),
                  _const_spec((ATTN_W, D_MODEL)), _const_spec((1, N_BRANCH * D_MODEL)),
                  tab, tab, tab],
        out_specs=[tok(CONV_CH), head, head,
                   pl.BlockSpec((None, N_HEADS, tm // tk, VT_ROWS, tk), lambda b, i: (b, 0, i, 0, 0)),
                   tok(POOL_CH), tok(N_BRANCH * D_MODEL)],
        out_shape=[jax.ShapeDtypeStruct((bsz, seq, CONV_CH), F32),
                   jax.ShapeDtypeStruct((bsz, N_HEADS, seq, LANES), BF16),
                   jax.ShapeDtypeStruct((bsz, N_HEADS, seq, LANES), BF16),
                   jax.ShapeDtypeStruct((bsz, N_HEADS, seq // tk, VT_ROWS, tk), BF16),
                   jax.ShapeDtypeStruct((bsz, seq, POOL_CH), F32),
                   jax.ShapeDtypeStruct((bsz, seq, N_BRANCH * D_MODEL), BF16)],
        compiler_params=_params(2),
        name="inproj",
    )(x, mod, mod, g_pre, w_rest, wvt, b_gate, cos, sa, sb)


def _attn_kernel(*refs, n_lat, lam_init):
    if n_lat:
        lamp_ref, gs_ref, q_ref, kc_ref, vtc_ref, k_ref, vt_ref, o_ref = refs
    else:
        lamp_ref, gs_ref, q_ref, kc_ref, vtc_ref, o_ref = refs
    tq = q_ref.shape[0]
    q = q_ref[...].astype(F32)
    lane = lax.broadcasted_iota(jnp.int32, q.shape, 1)
    first = (lane < HEAD_DIM).astype(F32)
    qs = jnp.concatenate([q * first, q * (1.0 - first)], axis=0).astype(BF16)

    def scores(keys):
        return [_dot_nt(kch, qs) for kch in keys]

    steps = [([kc_ref], [vtc_ref])]
    if n_lat:
        tk = k_ref.shape[0] // n_lat
        for j0 in range(0, n_lat, KEY_GROUP):
            js = range(j0, j0 + KEY_GROUP)
            steps.append(([k_ref.at[j * tk:(j + 1) * tk, :] for j in js],
                          [vt_ref.at[j] for j in js]))

    def pipelined(lookahead, consume, state):
        queue = [scores([r[...] for r in steps[i][0]]) for i in range(min(lookahead, len(steps)))]
        for i, (_, vts) in enumerate(steps):
            if i + lookahead < len(steps):
                queue.append(scores([r[...] for r in steps[i + lookahead][0]]))
            state = consume(state, queue.pop(0), [r[...] for r in vts])
        return state

    def online_step(state, ss, vts):
        m, acc = state
        m_new = m
        for s in ss:
            m_new = jnp.maximum(m_new, jnp.max(s, axis=0, keepdims=True))
        acc_new = jnp.exp2(m - m_new) * acc
        for s, vtch in zip(ss, vts):
            acc_new = acc_new + _dot(vtch, jnp.exp2(s - m_new).astype(BF16))
        return m_new, acc_new

    def exact():
        init = (jnp.full((1, 2 * tq), NEG_BIG, F32), jnp.zeros((VT_ROWS, 2 * tq), F32))
        return pipelined(LOOKAHEAD, online_step, init)[1]

    def fixed_step(state, ss, vts):
        m_ref, acc = state
        if m_ref is None:
            m_ref = jnp.max(ss[0], axis=0, keepdims=True)
        for s, vtch in zip(ss, vts):
            acc = acc + _dot(vtch, jnp.exp2(s - m_ref).astype(BF16))
        return m_ref, acc

    def optimistic():
        return pipelined(LOOKAHEAD, fixed_step,
                         (None, jnp.zeros((VT_ROWS, 2 * tq), F32)))[1]

    def finish(acc):
        lp = lamp_ref[...]
        lam = (jnp.exp(jnp.sum(lp[0:1] * lp[1:2], axis=1, keepdims=True))
               - jnp.exp(jnp.sum(lp[2:3] * lp[3:4], axis=1, keepdims=True)) + lam_init)
        on = acc[0:V_DIM] * (1.0 / acc[V_DIM:V_DIM + 1])
        o = on[:, :tq] - lam * on[:, tq:]
        ms = jnp.mean(o * o, axis=0, keepdims=True)
        o = (o * lax.rsqrt(ms + EPS)) * gs_ref[...]
        o = o * (1.0 - lam_init)
        o_ref[...] = o.T.astype(BF16)

    if n_lat:
        acc = optimistic()
        finish(acc)
        overflowed = jnp.logical_not(jnp.sum(acc * 0.0) == 0.0)

        @pl.when(overflowed)
        def _():
            finish(exact())
    else:
        finish(exact())


def _attention(q, kc, vtc, k, vt, lamp, gs, lam_init):
    bsz, _, sq, _ = q.shape
    n_lat = 0 if k is None else vt.shape[2]
    tq = min(TQ, sq)
    qspec = pl.BlockSpec((None, None, tq, LANES), lambda b, h, i: (b, h, i, 0))
    in_specs = [_const_spec((4, HEAD_DIM)), _const_spec((V_DIM, 1)), qspec,
                pl.BlockSpec((None, None, CTX_LEN, LANES), lambda b, h, i: (b, h, 0, 0)),
                pl.BlockSpec((None, None, None, VT_ROWS, CTX_LEN), lambda b, h, i: (b, h, 0, 0, 0))]
    args = [lamp, gs, q, kc, vtc]
    if n_lat:
        in_specs += [pl.BlockSpec((None, None, k.shape[2], LANES), lambda b, h, i: (b, h, 0, 0)),
                     pl.BlockSpec((None, None, n_lat, VT_ROWS, vt.shape[4]),
                                  lambda b, h, i: (b, h, 0, 0, 0))]
        args += [k, vt]
    return pl.pallas_call(
        functools.partial(_attn_kernel, n_lat=n_lat, lam_init=lam_init),
        grid=(bsz, N_HEADS, sq // tq),
        in_specs=in_specs,
        out_specs=qspec,
        out_shape=jax.ShapeDtypeStruct(q.shape, BF16),
        compiler_params=_params(3),
        name="attn_lat" if n_lat else "attn_ctx",
    )(*args)


def _convpool_kernel(u_ref, up_ref, un_ref, p_ref, pp_ref, pn_ref, cw_ref, cb_ref, lng_ref,
                     lnb_ref, wg_ref, ps_ref, a_ref, c_ref, uext, pext, pooled, ush, *, seq_len):
    i = pl.program_id(1)
    last = pl.num_programs(1) - 1
    keep_prev = (i > 0).astype(F32)
    keep_next = (i < last).astype(F32)
    for ext, cur, prev, nxt in ((uext, u_ref, up_ref, un_ref), (pext, p_ref, pp_ref, pn_ref)):
        ext[0:HALO, :] = prev[...] * keep_prev
        ext[HALO:HALO + TS, :] = cur[...]
        ext[HALO + TS:HALO + TS + HALO, :] = nxt[...] * keep_next

    n_sh = TS + 2 * HALO - SUBLANES
    for r in range(1, SUBLANES):
        ush[r - 1] = uext[r:r + n_sh, :]

    half = CONV_WIDTH // 2
    for r0 in range(0, TS, ROW_CHUNK):
        acc = jnp.zeros((ROW_CHUNK, CONV_CH), F32) + cb_ref[...]
        for kk in range(CONV_WIDTH):
            off = HALO + kk - half
            base = r0 + (off // SUBLANES) * SUBLANES
            if off % SUBLANES:
                rows = ush[off % SUBLANES - 1, base:base + ROW_CHUNK, :]
            else:
                rows = uext[base:base + ROW_CHUNK, :]
            acc = acc + cw_ref[kk:kk + 1, :] * rows
        mu = jnp.mean(acc, axis=-1, keepdims=True)
        d = acc - mu
        var = jnp.mean(d * d, axis=-1, keepdims=True)
        y = d * lax.rsqrt(var + EPS) * lng_ref[...] + lnb_ref[...]
        a_ref[r0:r0 + ROW_CHUNK, :] = (y * jax.nn.sigmoid(y)).astype(BF16)

        t = (i * TS + r0 + lax.broadcasted_iota(jnp.int32, (ROW_CHUNK, 1), 0))
        for gi, w in enumerate(POOL_WINDOWS):
            lanes = slice(gi * POOL_GROUP, (gi + 1) * POOL_GROUP)
            s = jnp.zeros((ROW_CHUNK, POOL_GROUP), F32)
            for j in range(w):
                start = HALO + r0 - w // 2 + j
                s = s + pext[start:start + ROW_CHUNK, lanes]
            cnt = (jnp.minimum(t + w // 2, seq_len) - jnp.maximum(t - w // 2, 0)).astype(F32)
            centre = pext[HALO + r0:HALO + r0 + ROW_CHUNK, lanes]
            pooled[r0:r0 + ROW_CHUNK, lanes] = s / cnt - centre

    mixed = _dot(pooled[...].astype(BF16), wg_ref[...])
    c_ref[...] = (mixed * ps_ref[...]).astype(BF16)


def _convpool(u, p, conv_w, conv_b, ln_g, ln_b, wg_bd, pool_scale):
    bsz, seq, _ = u.shape
    n_halo = seq // HALO
    per = TS // HALO
    cur = lambda: pl.BlockSpec((None, TS, CONV_CH), lambda b, i: (b, i, 0))
    prev = lambda: pl.BlockSpec((None, HALO, CONV_CH),
                                lambda b, i: (b, jnp.maximum(i * per - 1, 0), 0))
    nxt = lambda: pl.BlockSpec((None, HALO, CONV_CH),
                               lambda b, i: (b, jnp.minimum((i + 1) * per, n_halo - 1), 0))
    row = lambda: _const_spec((1, CONV_CH))
    return pl.pallas_call(
        functools.partial(_convpool_kernel, seq_len=seq),
        grid=(bsz, seq // TS),
        in_specs=[cur(), prev(), nxt(), cur(), prev(), nxt(),
                  _const_spec((32, CONV_CH)), row(), row(), row(),
                  _const_spec((POOL_CH, POOL_CH)), row()],
        out_specs=[cur(), cur()],
        out_shape=[jax.ShapeDtypeStruct((bsz, seq, CONV_CH), BF16),
                   jax.ShapeDtypeStruct((bsz, seq, POOL_CH), BF16)],
        scratch_shapes=[pltpu.VMEM((TS + 2 * HALO, CONV_CH), F32),
                        pltpu.VMEM((TS + 2 * HALO, POOL_CH), F32),
                        pltpu.VMEM((TS, POOL_CH), F32),
                        pltpu.VMEM((SUBLANES - 1, TS + 2 * HALO - SUBLANES, CONV_CH), F32)],
        compiler_params=_params(2),
        name="convpool",
    )(u, u, u, p, p, p, conv_w, conv_b, ln_g, ln_b, wg_bd, pool_scale)


def _merge_kernel(x_ref, gt_ref, gpost_ref, a_ref, o_ref, c_ref, gate_ref, wca_ref, wat_ref,
                  wpo_ref, wout_ref, xo_ref):
    ya = _dot(a_ref[...], wca_ref[...])
    o = jnp.concatenate([o_ref[hh] for hh in range(N_HEADS)], axis=1)
    yb = _dot(o, wat_ref[...])
    yc = _dot(c_ref[...], wpo_ref[...])
    m = (gate_ref[:, 0:D_MODEL] * ya + gate_ref[:, D_MODEL:2 * D_MODEL] * yb
         + gate_ref[:, 2 * D_MODEL:3 * D_MODEL] * yc)
    y = _dot(m.astype(BF16), wout_ref[...])
    ms = jnp.mean(y * y, axis=-1, keepdims=True)
    yn = (y * lax.rsqrt(ms + EPS)) * gpost_ref[...]
    xo_ref[...] = x_ref[...] + gt_ref[...] * yn


def _merge(x, mod, layer, row_fn, g_post, a, o, c, gate, wca, wat, wpo, wout, tm):
    bsz, seq, _ = x.shape
    tok = lambda width: pl.BlockSpec((None, tm, width), lambda b, i: (b, i, 0))
    return pl.pallas_call(
        _merge_kernel,
        grid=(bsz, seq // tm),
        in_specs=[tok(D_MODEL), _mod_spec(layer, row_fn, 2), _const_spec((1, D_MODEL)),
                  tok(CONV_CH),
                  pl.BlockSpec((None, N_HEADS, tm, LANES), lambda b, i: (b, 0, i, 0)),
                  tok(POOL_CH), tok(N_BRANCH * D_MODEL),
                  _const_spec((CONV_CH, D_MODEL)), _const_spec((ATTN_W, D_MODEL)),
                  _const_spec((POOL_CH, D_MODEL)), _const_spec((D_MODEL, D_MODEL))],
        out_specs=tok(D_MODEL),
        out_shape=jax.ShapeDtypeStruct(x.shape, F32),
        compiler_params=_params(2),
        name="merge",
    )(x, mod, g_post, a, o, c, gate, wca, wat, wpo, wout)


def _ffn_kernel(x_ref, sc_ref, sh_ref, gt_ref, gpre_ref, gpost_ref, w1_ref, w2_ref, xo_ref):
    x = x_ref[...]
    ms = jnp.mean(x * x, axis=-1, keepdims=True)
    h = (x * lax.rsqrt(ms + EPS)) * gpre_ref[...]
    hb = (h * (1.0 + sc_ref[...]) + sh_ref[...]).astype(BF16)
    y = jnp.zeros(x.shape, F32)
    lo = 0
    for width in FFN_CHUNKS:
        u1 = _dot(hb, w1_ref[:, lo:lo + width])
        u2 = _dot(hb, w1_ref[:, D_FF + lo:D_FF + lo + width])
        act = ((u1 * jax.nn.sigmoid(u1)) * u2).astype(BF16)
        y = y + _dot(act, w2_ref[lo:lo + width, :])
        lo += width
    ms2 = jnp.mean(y * y, axis=-1, keepdims=True)
    yn = (y * lax.rsqrt(ms2 + EPS)) * gpost_ref[...]
    xo_ref[...] = x + gt_ref[...] * yn


def _ffn(x, mod, layer, row_fn, g_pre, g_post, w1, w2, tm):
    bsz, seq, _ = x.shape
    tok = pl.BlockSpec((None, tm, D_MODEL), lambda b, i: (b, i, 0))
    return pl.pallas_call(
        _ffn_kernel,
        grid=(bsz, seq // tm),
        in_specs=[tok, _mod_spec(layer, row_fn, 4), _mod_spec(layer, row_fn, 3),
                  _mod_spec(layer, row_fn, 5), _const_spec((1, D_MODEL)), _const_spec((1, D_MODEL)),
                  _const_spec((D_MODEL, 2 * D_FF)), _const_spec((D_FF, D_MODEL))],
        out_specs=tok,
        out_shape=jax.ShapeDtypeStruct(x.shape, F32),
        compiler_params=_params(2),
        name="ffn",
    )(x, mod, mod, mod, g_pre, g_post, w1, w2)


def _rope_tables(n_tokens):
    rows = n_tokens // GRID_W
    row = jnp.repeat(jnp.arange(rows), GRID_W).astype(F32)
    col = jnp.tile(jnp.arange(GRID_W), rows).astype(F32)
    half = HEAD_DIM // 2
    inv = ROPE_BASE ** (-jnp.arange(0, half, 2, dtype=F32) / half)
    lane = jnp.arange(LANES)
    d = lane % HEAD_DIM
    freq = inv[d % (half // 2)]
    pos = jnp.where((d < half)[None, :], row[:, None], col[:, None])
    ang = pos * freq[None, :]
    first = ((d % half) < half // 2)[None, :]
    sin = jnp.sin(ang)
    return jnp.cos(ang), jnp.where(first, -sin, 0.0), jnp.where(first, 0.0, sin)


def kernel(x, c, ctx, c_ctx, w_mod, b_mod, g_pre_mix, g_post_mix, w_in, b_gate, conv_w, conv_b, conv_ln_g, conv_ln_b, w_conv_out, lam_q1, lam_k1, lam_q2, lam_k2, subln_g, w_attn_out, w_pool_group, pool_scale, w_pool_out, w_out, g_pre_ffn, g_post_ffn, w_ffn_in, w_ffn_out):
    bsz, seq, _ = x.shape
    depth = w_mod.shape[0]
    assert ctx.shape[1] == CTX_LEN and seq % TM_LAT == 0 and seq % GRID_W == 0
    assert bsz + 1 <= MOD_ROWS

    c_all = jnp.zeros((MOD_ROWS, D_MODEL), F32).at[:bsz].set(c).at[bsz].set(c_ctx)
    mod = _modulation(c_all, w_mod, b_mod).reshape(depth, MOD_ROWS, 1, N_MOD * D_MODEL)
    lat_row = lambda b: b
    ctx_row = lambda b: bsz

    rope_lat = _rope_tables(seq)
    rope_ctx = (jnp.ones((CTX_LEN, LANES), F32), jnp.zeros((CTX_LEN, LANES), F32),
                jnp.zeros((CTX_LEN, LANES), F32))
    row2 = lambda v: v.reshape(1, -1)

    xl, xc = x, ctx
    for l in range(depth):
        need_ctx = l < depth - 1
        lam_init = 0.8 - 0.6 * math.exp(-0.3 * l)
        wl = w_in[l]
        w_rest = jnp.concatenate([wl[:, :COL_V], wl[:, COL_P:]], axis=1).astype(BF16)
        wvt = wl[:, COL_V:COL_P].T.astype(BF16)
        bg = row2(b_gate[l])
        lamp = jnp.stack([lam_q1[l], lam_k1[l], lam_q2[l], lam_k2[l]])
        gs = subln_g[l].reshape(V_DIM, 1)
        cw = jnp.zeros((32, CONV_CH), F32).at[:CONV_WIDTH].set(conv_w[l])
        wg_bd = jax.scipy.linalg.block_diag(*[w_pool_group[l, g] for g in range(len(POOL_WINDOWS))]).astype(BF16)
        wca, wat = w_conv_out[l].astype(BF16), w_attn_out[l].astype(BF16)
        wpo, wout = w_pool_out[l].astype(BF16), w_out[l].astype(BF16)
        w1, w2 = w_ffn_in[l].astype(BF16), w_ffn_out[l].astype(BF16)
        gpm, gqm = row2(g_pre_mix[l]), row2(g_post_mix[l])
        gpf, gqf = row2(g_pre_ffn[l]), row2(g_post_ffn[l])
        conv_args = (cw, row2(conv_b[l]), row2(conv_ln_g[l]), row2(conv_ln_b[l]), wg_bd,
                     row2(pool_scale[l]))

        uc, qc, kc, vtc, pc, gc = _inproj(xc, mod, l, ctx_row, gpm, w_rest, wvt, bg, rope_ctx,
                                          CTX_LEN, CTX_LEN)
        ul, ql, kl, vtl, pl_, gl = _inproj(xl, mod, l, lat_row, gpm, w_rest, wvt, bg, rope_lat,
                                           TM_LAT, TK)
        o_l = _attention(ql, kc, vtc, kl, vtl, lamp, gs, lam_init)
        a_l, c_l = _convpool(ul, pl_, *conv_args)
        xl = _merge(xl, mod, l, lat_row, gqm, a_l, o_l, c_l, gl, wca, wat, wpo, wout, TM_LAT)
        xl = _ffn(xl, mod, l, lat_row, gpf, gqf, w1, w2, TM_LAT)
        if need_ctx:
            o_c = _attention(qc, kc, vtc, None, None, lamp, gs, lam_init)
            a_c, c_c = _convpool(uc, pc, *conv_args)
            xc = _merge(xc, mod, l, ctx_row, gqm, a_c, o_c, c_c, gc, wca, wat, wpo, wout, CTX_LEN)
            xc = _ffn(xc, mod, l, ctx_row, gpf, gqf, w1, w2, CTX_LEN)
    return xl
```
